```python
import math
import jax, jax.numpy as jnp
from jax import lax
import numpy as np

D_MODEL = 2048
BATCH = 2
SEQ = 16384
DEPTH = 2

MEM_LEN = 256
BLOCK = 128
EPS = 1e-6
NEG = -1e30

MLA_HEADS = 8
MLA_Q_RANK = 768
MLA_KV_RANK = 512
MLA_NOPE = 128
MLA_ROPE = 64
MLA_V = 128
ROPE_THETA = 10000.0

SWA_HEADS = 16
SWA_KV_HEADS = 2
SWA_GROUP = SWA_HEADS // SWA_KV_HEADS
SWA_HEAD_DIM = 64
WINDOW = 128

REL_BUCKETS = 32
REL_MAX_DIST = 128

FOX_HEADS = 16
FOX_HEAD_DIM = 128
FOX_WIDTH = FOX_HEADS * FOX_HEAD_DIM

X_HEADS = 4
X_HEAD_DIM = 128
X_WIDTH = X_HEADS * X_HEAD_DIM

D_FF = 7168
N_EXPERTS = 8
TOP_K = 2

N_EVEN = (DEPTH + 1) // 2
N_ODD = DEPTH // 2

EVEN_SPLITS = (
    MLA_Q_RANK,
    MLA_Q_RANK + MLA_KV_RANK,
    MLA_Q_RANK + MLA_KV_RANK + MLA_ROPE,
    MLA_Q_RANK + MLA_KV_RANK + MLA_ROPE + SWA_HEADS * SWA_HEAD_DIM,
    MLA_Q_RANK + MLA_KV_RANK + MLA_ROPE + (SWA_HEADS + SWA_KV_HEADS) * SWA_HEAD_DIM,
)
EVEN_IN = MLA_Q_RANK + MLA_KV_RANK + MLA_ROPE + (SWA_HEADS + 2 * SWA_KV_HEADS) * SWA_HEAD_DIM
EVEN_OUT = MLA_HEADS * MLA_V + SWA_HEADS * SWA_HEAD_DIM
ODD_SPLITS = (FOX_WIDTH, 2 * FOX_WIDTH, 3 * FOX_WIDTH)
ODD_IN = 3 * FOX_WIDTH + FOX_HEADS

kernel_name = "hybrid_mla_swa_fox_moe_trunk"


def rmsnorm(x, g):
    xf = x.astype(jnp.float32)
    y = xf * lax.rsqrt(jnp.mean(xf * xf, axis=-1, keepdims=True) + EPS)
    return (y * g.astype(jnp.float32)).astype(x.dtype)


def rope(x, positions):
    half = x.shape[-1] // 2
    inv = ROPE_THETA ** (-jnp.arange(half, dtype=jnp.float32) / half)
    ang = positions.astype(jnp.float32)[..., None] * inv
    ang = ang.reshape(ang.shape[:2] + (1,) * (x.ndim - 3) + (half,))
    cos, sin = jnp.cos(ang), jnp.sin(ang)
    xf = x.astype(jnp.float32)
    x1, x2 = xf[..., :half], xf[..., half:]
    return jnp.concatenate([x1 * cos - x2 * sin, x2 * cos + x1 * sin], axis=-1).astype(x.dtype)


def t5_bucket(dist):
    max_exact = REL_BUCKETS // 2
    large = max_exact + (jnp.log(jnp.maximum(dist, 1).astype(jnp.float32) / max_exact)
                         / math.log(REL_MAX_DIST / max_exact) * (REL_BUCKETS - max_exact)).astype(jnp.int32)
    large = jnp.minimum(large, REL_BUCKETS - 1)
    return jnp.where(dist < max_exact, dist, large)


def causal_block_attention(q, k, v, scale, decay=None):
    B, S, H, _ = q.shape
    Dv = v.shape[-1]
    nb = S // BLOCK
    kpos = jnp.arange(S)
    if decay is not None:
        dk = jnp.swapaxes(decay, 1, 2)

    def step(n):
        start = n * BLOCK
        qblk = lax.dynamic_slice_in_dim(q, start, BLOCK, axis=1)
        s = jnp.einsum('bqhd,bshd->bhqs', qblk, k).astype(jnp.float32) * scale
        if decay is not None:
            dq = lax.dynamic_slice_in_dim(decay, start, BLOCK, axis=1)
            s = s + jnp.swapaxes(dq, 1, 2)[..., None] - dk[:, :, None, :]
        qpos = start + jnp.arange(BLOCK)
        s = jnp.where(kpos[None, :] <= qpos[:, None], s, NEG)
        p = jax.nn.softmax(s, axis=-1).astype(v.dtype)
        return jnp.einsum('bhqs,bshd->bqhd', p, v)

    out = lax.map(step, jnp.arange(nb))
    return jnp.moveaxis(out, 0, 1).reshape(B, S, H * Dv)


def sliding_window_attention(q, k, v, sinks, rel_bias):
    B, S = q.shape[:2]
    nb = S // BLOCK
    qb = q.reshape(B, nb, BLOCK, SWA_KV_HEADS, SWA_GROUP, SWA_HEAD_DIM)

    def band(a):
        a = a.reshape(B, nb, BLOCK, SWA_KV_HEADS, SWA_HEAD_DIM)
        prev = jnp.pad(a, ((0, 0), (1, 0), (0, 0), (0, 0), (0, 0)))[:, :-1]
        return jnp.concatenate([prev, a], axis=2)

    kb, vb = band(k), band(v)
    s = jnp.einsum('bnqkgd,bnskd->bnkgqs', qb, kb).astype(jnp.float32) * (SWA_HEAD_DIM ** -0.5)
    i = jnp.arange(BLOCK)[:, None]
    j = jnp.arange(2 * BLOCK)[None, :]
    rel = BLOCK + i - j
    bias = rel_bias[t5_bucket(jnp.maximum(rel, 0))].astype(jnp.float32)
    bias = jnp.transpose(bias, (2, 0, 1)).reshape(SWA_KV_HEADS, SWA_GROUP, BLOCK, 2 * BLOCK)
    s_abs = (jnp.arange(nb)[:, None, None] - 1) * BLOCK + j
    valid = (rel >= 0) & (rel < WINDOW) & (s_abs >= 0)
    s = jnp.where(valid[None, :, None, None], s + bias, NEG)
    sink = jnp.broadcast_to(sinks.astype(jnp.float32).reshape(1, 1, SWA_KV_HEADS, SWA_GROUP, 1, 1),
                            s.shape[:-1] + (1,))
    p = jax.nn.softmax(jnp.concatenate([s, sink], axis=-1), axis=-1)[..., :-1].astype(v.dtype)
    o = jnp.einsum('bnkgqs,bnskd->bnqkgd', p, vb)
    return o.reshape(B, S, SWA_HEADS * SWA_HEAD_DIM)


def even_mixer(h, positions, rel_bias, w_in, q_norm, w_uq, kv_norm, w_ukv, sinks, w_o):
    B, S, _ = h.shape
    z = h @ w_in
    c_q, c_kv, k_rope, q_s, k_s, v_s = jnp.split(z, EVEN_SPLITS, axis=-1)
    q = (rmsnorm(c_q, q_norm) @ w_uq).reshape(B, S, MLA_HEADS, MLA_NOPE + MLA_ROPE)
    q = jnp.concatenate([q[..., :MLA_NOPE], rope(q[..., MLA_NOPE:], positions)], axis=-1)
    kv = (rmsnorm(c_kv, kv_norm) @ w_ukv).reshape(B, S, MLA_HEADS, MLA_NOPE + MLA_V)
    k_pe = jnp.broadcast_to(rope(k_rope, positions)[:, :, None, :], (B, S, MLA_HEADS, MLA_ROPE))
    k = jnp.concatenate([kv[..., :MLA_NOPE], k_pe], axis=-1)
    o_mla = causal_block_attention(q, k, kv[..., MLA_NOPE:], (MLA_NOPE + MLA_ROPE) ** -0.5)
    o_swa = sliding_window_attention(
        q_s.reshape(B, S, SWA_HEADS, SWA_HEAD_DIM),
        k_s.reshape(B, S, SWA_KV_HEADS, SWA_HEAD_DIM),
        v_s.reshape(B, S, SWA_KV_HEADS, SWA_HEAD_DIM),
        sinks, rel_bias)
    return jnp.concatenate([o_mla, o_swa], axis=-1) @ w_o


def odd_mixer(h, w_in, b_f, w_o):
    B, S, _ = h.shape
    z = h @ w_in
    q, k, v, f_logit = jnp.split(z, ODD_SPLITS, axis=-1)
    log_f = jax.nn.log_sigmoid(f_logit.astype(jnp.float32) + b_f.astype(jnp.float32))
    decay = jnp.cumsum(log_f, axis=1)
    shp = (B, S, FOX_HEADS, FOX_HEAD_DIM)
    o = causal_block_attention(q.reshape(shp), k.reshape(shp), v.reshape(shp),
                               FOX_HEAD_DIM ** -0.5, decay)
    return o @ w_o


def cross_attention(h, mem_n, w_q, w_k, w_v, w_o):
    B, S, _ = h.shape
    M = mem_n.shape[1]
    q = (h @ w_q).reshape(B, S, X_HEADS, X_HEAD_DIM)
    k = (mem_n @ w_k).reshape(B, M, X_HEADS, X_HEAD_DIM)
    v = (mem_n @ w_v).reshape(B, M, X_HEADS, X_HEAD_DIM)
    s = jnp.einsum('bqhd,bmhd->bhqm', q, k).astype(jnp.float32) * (X_HEAD_DIM ** -0.5)
    p = jax.nn.softmax(s, axis=-1).astype(v.dtype)
    return jnp.einsum('bhqm,bmhd->bqhd', p, v).reshape(B, S, X_WIDTH) @ w_o


def swiglu(h, w_gate, w_up, w_down):
    return (jax.nn.silu(h @ w_gate) * (h @ w_up)) @ w_down


def moe_swiglu(h, w_router, b_router, w_gate, w_up, w_down):
    logits = (h @ w_router).astype(jnp.float32) + b_router.astype(jnp.float32)
    top_vals, top_idx = lax.top_k(logits, TOP_K)
    gates = jax.nn.softmax(top_vals, axis=-1)
    combine = jnp.sum(jax.nn.one_hot(top_idx, N_EXPERTS, dtype=jnp.float32) * gates[..., None], axis=-2)
    out = jnp.zeros_like(h)
    for e in range(N_EXPERTS):
        out = out + swiglu(h, w_gate[e], w_up[e], w_down[e]) * combine[..., e:e + 1].astype(h.dtype)
    return out


def setup_inputs(seed: int = 0) -> dict:
    key = jax.random.key(seed)
    ks = iter(jax.random.split(key, 64))
    f32 = jnp.float32

    def w(shape, fan_in):
        return jax.random.normal(next(ks), shape, f32) * (fan_in ** -0.5)

    def gain(shape):
        return 1.0 + 0.05 * jax.random.normal(next(ks), shape, f32)

    x = jax.random.normal(next(ks), (BATCH, SEQ, D_MODEL), f32)
    mem = jax.random.normal(next(ks), (BATCH, MEM_LEN, D_MODEL), f32)
    offset = jax.random.randint(next(ks), (BATCH, 1), 0, 4096, dtype=jnp.int32)
    positions = offset + jnp.arange(SEQ, dtype=jnp.int32)[None, :]
    return {
        "x": x,
        "mem": mem,
        "positions": positions,
        "rel_bias": 0.5 * jax.random.normal(next(ks), (REL_BUCKETS, SWA_HEADS), f32),
        "norm_mix": gain((DEPTH, D_MODEL)),
        "norm_cross": gain((DEPTH, D_MODEL)),
        "norm_mem": gain((DEPTH, D_MODEL)),
        "norm_ffn": gain((DEPTH, D_MODEL)),
        "x_wq": w((DEPTH, D_MODEL, X_WIDTH), D_MODEL),
        "x_wk": w((DEPTH, D_MODEL, X_WIDTH), D_MODEL),
        "x_wv": w((DEPTH, D_MODEL, X_WIDTH), D_MODEL),
        "x_wo": w((DEPTH, X_WIDTH, D_MODEL), X_WIDTH),
        "ev_w_in": w((N_EVEN, D_MODEL, EVEN_IN), D_MODEL),
        "ev_mla_q_norm": gain((N_EVEN, MLA_Q_RANK)),
        "ev_mla_w_uq": w((N_EVEN, MLA_Q_RANK, MLA_HEADS * (MLA_NOPE + MLA_ROPE)), MLA_Q_RANK),
        "ev_mla_kv_norm": gain((N_EVEN, MLA_KV_RANK)),
        "ev_mla_w_ukv": w((N_EVEN, MLA_KV_RANK, MLA_HEADS * (MLA_NOPE + MLA_V)), MLA_KV_RANK),
        "ev_swa_sinks": jax.random.normal(next(ks), (N_EVEN, SWA_HEADS), f32),
        "ev_w_o": w((N_EVEN, EVEN_OUT, D_MODEL), EVEN_OUT),
        "ev_ffn_w_gate": w((N_EVEN, D_MODEL, D_FF), D_MODEL),
        "ev_ffn_w_up": w((N_EVEN, D_MODEL, D_FF), D_MODEL),
        "ev_ffn_w_down": w((N_EVEN, D_FF, D_MODEL), D_FF),
        "od_w_in": w((N_ODD, D_MODEL, ODD_IN), D_MODEL),
        "od_fox_b_f": jax.random.uniform(next(ks), (N_ODD, FOX_HEADS), f32, 1.0, 6.0),
        "od_w_o": w((N_ODD, FOX_WIDTH, D_MODEL), FOX_WIDTH),
        "od_router_w": w((N_ODD, D_MODEL, N_EXPERTS), D_MODEL),
        "od_router_b": 0.01 * jax.random.normal(next(ks), (N_ODD, N_EXPERTS), f32),
        "od_moe_w_gate": w((N_ODD, N_EXPERTS, D_MODEL, D_FF), D_MODEL),
        "od_moe_w_up": w((N_ODD, N_EXPERTS, D_MODEL, D_FF), D_MODEL),
        "od_moe_w_down": w((N_ODD, N_EXPERTS, D_FF, D_MODEL), D_FF),
        "final_norm": gain((D_MODEL,)),
    }


def reference(x, mem, positions, rel_bias, norm_mix, norm_cross, norm_mem, norm_ffn,
              x_wq, x_wk, x_wv, x_wo,
              ev_w_in, ev_mla_q_norm, ev_mla_w_uq, ev_mla_kv_norm, ev_mla_w_ukv,
              ev_swa_sinks, ev_w_o, ev_ffn_w_gate, ev_ffn_w_up, ev_ffn_w_down,
              od_w_in, od_fox_b_f, od_w_o, od_router_w, od_router_b,
              od_moe_w_gate, od_moe_w_up, od_moe_w_down, final_norm):
    for layer in range(DEPTH):
        i = layer // 2
        h = rmsnorm(x, norm_mix[layer])
        if layer % 2 == 0:
            x = x + even_mixer(h, positions, rel_bias, ev_w_in[i], ev_mla_q_norm[i], ev_mla_w_uq[i],
                               ev_mla_kv_norm[i], ev_mla_w_ukv[i], ev_swa_sinks[i], ev_w_o[i])
        else:
            x = x + odd_mixer(h, od_w_in[i], od_fox_b_f[i], od_w_o[i])
        x = x + cross_attention(rmsnorm(x, norm_cross[layer]), rmsnorm(mem, norm_mem[layer]),
                                x_wq[layer], x_wk[layer], x_wv[layer], x_wo[layer])
        h = rmsnorm(x, norm_ffn[layer])
        if layer % 2 == 0:
            x = x + swiglu(h, ev_ffn_w_gate[i], ev_ffn_w_up[i], ev_ffn_w_down[i])
        else:
            x = x + moe_swiglu(h, od_router_w[i], od_router_b[i],
                               od_moe_w_gate[i], od_moe_w_up[i], od_moe_w_down[i])
    return rmsnorm(x, final_norm)
```

```python
import functools
import math

import jax
import jax.numpy as jnp
from jax import lax
from jax.experimental import pallas as pl
from jax.experimental.pallas import tpu as pltpu

F32 = jnp.float32
BF16 = jnp.bfloat16

EPS = 1e-6
NEG = -1e30
LANES = 128

MLA_HEADS = 8
MLA_Q_RANK = 768
MLA_KV_RANK = 512
MLA_NOPE = 128
MLA_ROPE = 64
MLA_V = 128
ROPE_THETA = 10000.0
SWA_HEADS = 16
SWA_KV_HEADS = 2
SWA_GROUP = SWA_HEADS // SWA_KV_HEADS
SWA_HEAD_DIM = 64
WINDOW = 128
REL_BUCKETS = 32
REL_MAX_DIST = 128
FOX_HEADS = 16
FOX_HEAD_DIM = 128
X_HEADS = 4
X_HEAD_DIM = 128
N_EXPERTS = 8

VMEM_LIMIT_BYTES = 56 * 1024 * 1024


def _tile(n, pref):
    return pref if n % pref == 0 else n


def _params(*sem):
    return pltpu.CompilerParams(dimension_semantics=sem, vmem_limit_bytes=VMEM_LIMIT_BYTES)


def _rms(xf, g):
    ms = jnp.mean(xf * xf, axis=-1, keepdims=True)
    return xf * lax.rsqrt(ms + EPS) * g


def _split3(v):
    hi = v.astype(BF16)
    r = v - hi.astype(F32)
    mid = r.astype(BF16)
    lo = (r - mid.astype(F32)).astype(BF16)
    return hi, mid, lo


def _dot(a, b):
    return jnp.dot(a, b, preferred_element_type=F32)


def _dot_nt(a, b):
    return lax.dot_general(a, b, (((1,), (1,)), ((), ())), preferred_element_type=F32)


def _norm_matmul_kernel(x_ref, g_ref, w_ref, cs_ref, *rest, side):
    if side:
        wsh_ref, wsl_ref, o_ref, so_ref, h_sc = rest
    else:
        o_ref, h_sc = rest

    @pl.when(pl.program_id(1) == 0)
    def _():
        hn = _rms(x_ref[...], g_ref[...])
        h_hi = hn.astype(BF16)
        h_sc[...] = h_hi
        if side:
            h_lo = (hn - h_hi.astype(F32)).astype(BF16)
            so_ref[...] = (_dot(h_hi, wsh_ref[...]) + _dot(h_lo, wsh_ref[...])
                           + _dot(h_hi, wsl_ref[...]))

    o_ref[...] = (_dot(h_sc[...], w_ref[...]) * cs_ref[...]).astype(o_ref.dtype)


def norm_matmul(x, g, w, col_scale=None, side_w=None, tm=512, tn=2048):
    T, D = x.shape
    N = w.shape[1]
    tm, tn = _tile(T, tm), _tile(N, tn)
    if col_scale is None:
        col_scale = jnp.ones((1, N), F32)
    in_specs = [pl.BlockSpec((tm, D), lambda i, j: (i, 0)),
                pl.BlockSpec((1, D), lambda i, j: (0, 0)),
                pl.BlockSpec((D, tn), lambda i, j: (0, j)),
                pl.BlockSpec((1, tn), lambda i, j: (0, j))]
    args = [x, g.reshape(1, D), w, col_scale]
    out_shape = [jax.ShapeDtypeStruct((T, N), BF16)]
    out_specs = [pl.BlockSpec((tm, tn), lambda i, j: (i, j))]
    side = side_w is not None
    if side:
        ws_hi = side_w.astype(BF16)
        ws_lo = (side_w - ws_hi.astype(F32)).astype(BF16)
        ns = side_w.shape[1]
        in_specs += [pl.BlockSpec((D, ns), lambda i, j: (0, 0))] * 2
        args += [ws_hi, ws_lo]
        out_shape.append(jax.ShapeDtypeStruct((T, ns), F32))
        out_specs.append(pl.BlockSpec((tm, ns), lambda i, j: (i, 0)))
    res = pl.pallas_call(
        functools.partial(_norm_matmul_kernel, side=side),
        grid=(T // tm, N // tn),
        in_specs=in_specs, out_specs=out_specs, out_shape=out_shape,
        scratch_shapes=[pltpu.VMEM((tm, D), BF16)],
        compiler_params=_params("parallel", "arbitrary"),
        name="norm_matmul",
    )(*args)
    return res if side else res[0]


def _matmul_res_kernel(*refs, n):
    x_ref = refs[0]
    o_ref = refs[1 + 2 * n]
    acc = x_ref[...]
    for k in range(n):
        acc = acc + _dot(refs[1 + 2 * k][...], refs[2 + 2 * k][...])
    o_ref[...] = acc


def matmul_res(x, pairs, tm=512, tn=1024):
    T, N = x.shape
    tm, tn = _tile(T, tm), _tile(N, tn)
    in_specs = [pl.BlockSpec((tm, tn), lambda i, j: (i, j))]
    args = [x]
    for a, w in pairs:
        K = a.shape[1]
        in_specs += [pl.BlockSpec((tm, K), lambda i, j: (i, 0)),
                     pl.BlockSpec((K, tn), lambda i, j: (0, j))]
        args += [a, w]
    return pl.pallas_call(
        functools.partial(_matmul_res_kernel, n=len(pairs)),
        grid=(T // tm, N // tn),
        in_specs=in_specs,
        out_specs=pl.BlockSpec((tm, tn), lambda i, j: (i, j)),
        out_shape=jax.ShapeDtypeStruct((T, N), F32),
        compiler_params=_params("parallel", "parallel"),
        name="matmul_res",
    )(*args)


def _mla_prep_kernel(z_ref, cos_ref, sin_ref, qg_ref, kvg_ref, wq_ref, wqs_ref, wk_ref, wv_ref,
                     q_ref, k_ref, v_ref, *, scale):
    cos2 = cos_ref[...]
    sin2 = sin_ref[...]
    cq = _rms(z_ref[:, 512:1280].astype(F32), qg_ref[...]).astype(BF16)
    qa = _dot(cq, wq_ref[...])
    qs = _dot(cq, wqs_ref[...])
    for h in range(MLA_HEADS):
        a, b = h * 256, h * 256 + 128
        q_ref[:, a:b] = (qa[:, a:b] * scale).astype(BF16)
        q_ref[:, b:b + 128] = ((qa[:, b:b + 128] * cos2 + qs[:, h * 128:(h + 1) * 128] * sin2)
                               * scale).astype(BF16)
    ckv = _rms(z_ref[:, 1280:1792].astype(F32), kvg_ref[...]).astype(BF16)
    kn = _dot(ckv, wk_ref[...])
    v_ref[...] = _dot(ckv, wv_ref[...]).astype(BF16)
    kpe = (z_ref[:, 1792:1920].astype(F32) * cos2 + z_ref[:, 1920:2048].astype(F32) * sin2).astype(BF16)
    for h in range(MLA_HEADS):
        k_ref[:, h * 256:h * 256 + 128] = kn[:, h * 128:(h + 1) * 128].astype(BF16)
        k_ref[:, h * 256 + 128:(h + 1) * 256] = kpe


def mla_prep(z, cos2, sin2, q_norm, kv_norm, wq_all, wq_swap, wk, wv, tm=512):
    T = z.shape[0]
    tm = _tile(T, tm)
    full = lambda a: pl.BlockSpec(a.shape, lambda i: (0, 0))
    qg, kvg = q_norm.reshape(1, -1), kv_norm.reshape(1, -1)
    return pl.pallas_call(
        functools.partial(_mla_prep_kernel, scale=(MLA_NOPE + MLA_ROPE) ** -0.5),
        grid=(T // tm,),
        in_specs=[pl.BlockSpec((tm, 2048), lambda i: (i, 1)),
                  pl.BlockSpec((tm, LANES), lambda i: (i, 0)),
                  pl.BlockSpec((tm, LANES), lambda i: (i, 0)),
                  full(qg), full(kvg), full(wq_all), full(wq_swap), full(wk), full(wv)],
        out_specs=[pl.BlockSpec((tm, MLA_HEADS * 256), lambda i: (i, 0)),
                   pl.BlockSpec((tm, MLA_HEADS * 256), lambda i: (i, 0)),
                   pl.BlockSpec((tm, MLA_HEADS * MLA_V), lambda i: (i, 0))],
        out_shape=[jax.ShapeDtypeStruct((T, MLA_HEADS * 256), BF16),
                   jax.ShapeDtypeStruct((T, MLA_HEADS * 256), BF16),
                   jax.ShapeDtypeStruct((T, MLA_HEADS * MLA_V), BF16)],
        compiler_params=_params("parallel"),
        name="mla_prep",
    )(z, cos2, sin2, qg, kvg, wq_all, wq_swap, wk, wv)


def _flash_kernel(*refs, tq, has_bias):
    if has_bias:
        q_ref, k_ref, v_ref, kb_ref, o_ref, m_sc, l_sc, acc_sc = refs
    else:
        q_ref, k_ref, v_ref, o_ref, m_sc, l_sc, acc_sc = refs
        kb_ref = None
    i = pl.program_id(2)
    q = q_ref[...]
    m_sc[...] = jnp.full(m_sc.shape, NEG, F32)
    l_sc[...] = jnp.zeros(l_sc.shape, F32)
    acc_sc[...] = jnp.zeros(acc_sc.shape, F32)

    def chunk(j, masked):
        start = pl.multiple_of(j * tq, tq)
        k = k_ref[pl.ds(start, tq), :]
        v = v_ref[pl.ds(start, tq), :]
        s = _dot_nt(q, k)
        if has_bias:
            s = s - kb_ref[pl.ds(j, 1), :]
        if masked:
            row = lax.broadcasted_iota(jnp.int32, (tq, tq), 0)
            col = lax.broadcasted_iota(jnp.int32, (tq, tq), 1)
            s = jnp.where(col <= row, s, NEG)
        m_prev = m_sc[...]
        m_new = jnp.maximum(m_prev, jnp.max(s, axis=-1, keepdims=True))
        alpha = jnp.exp(m_prev - m_new)
        p = jnp.exp(s - m_new)
        l_sc[...] = alpha * l_sc[...] + jnp.sum(p, axis=-1, keepdims=True)
        acc_sc[...] = alpha * acc_sc[...] + _dot(p.astype(BF16), v)
        m_sc[...] = m_new

    def body(j, c):
        chunk(j, False)
        return c

    lax.fori_loop(0, i, body, 0)
    chunk(i, True)
    o_ref[...] = (acc_sc[...] / l_sc[...]).astype(o_ref.dtype)


def flash_attention(q_arr, k_arr, v_arr, kbias, *, heads, dk, dv, q_off, k_off, v_off, tq=512):
    B, S = q_arr.shape[:2]
    tq = _tile(S, tq)
    has_bias = kbias is not None
    in_specs = [pl.BlockSpec((None, tq, dk), lambda b, h, i: (b, i, q_off + h)),
                pl.BlockSpec((None, S, dk), lambda b, h, i: (b, 0, k_off + h)),
                pl.BlockSpec((None, S, dv), lambda b, h, i: (b, 0, v_off + h))]
    args = [q_arr, k_arr, v_arr]
    if has_bias:
        in_specs.append(pl.BlockSpec((None, None, S // tq, tq), lambda b, h, i: (b, h, 0, 0)))
        args.append(kbias)
    return pl.pallas_call(
        functools.partial(_flash_kernel, tq=tq, has_bias=has_bias),
        grid=(B, heads, S // tq),
        in_specs=in_specs,
        out_specs=pl.BlockSpec((None, tq, dv), lambda b, h, i: (b, i, h)),
        out_shape=jax.ShapeDtypeStruct((B, S, heads * dv), BF16),
        scratch_shapes=[pltpu.VMEM((tq, 1), F32), pltpu.VMEM((tq, 1), F32), pltpu.VMEM((tq, dv), F32)],
        compiler_params=_params("parallel", "parallel", "arbitrary"),
        name="flash_attention",
    )(*args)


def _swa_kernel(sink_ref, q_ref, kvc_ref, kvp_ref, bias_ref, o_ref):
    i = pl.program_id(1)
    lane = lax.broadcasted_iota(jnp.int32, (1, 2 * WINDOW), 1)
    pen = jnp.where((lane < WINDOW) & (i == 0), NEG, 0.0).astype(F32)
    for g in range(SWA_KV_HEADS):
        kcat = jnp.concatenate([kvp_ref[:, g * 128:(g + 1) * 128], kvc_ref[:, g * 128:(g + 1) * 128]], axis=0)
        vcat = jnp.concatenate([kvp_ref[:, 256 + g * 128:256 + (g + 1) * 128],
                                kvc_ref[:, 256 + g * 128:256 + (g + 1) * 128]], axis=0)
        for hh in range(SWA_GROUP):
            h = g * SWA_GROUP + hh
            s = _dot_nt(q_ref[:, h * 128:(h + 1) * 128], kcat) * (SWA_HEAD_DIM ** -0.5) + bias_ref[h] + pen
            sink = sink_ref[h]
            m = jnp.maximum(jnp.max(s, axis=-1, keepdims=True), sink)
            p = jnp.exp(s - m)
            denom = jnp.sum(p, axis=-1, keepdims=True) + jnp.exp(sink - m)
            o = _dot(p.astype(BF16), vcat) / denom
            o_ref[:, h * 128:(h + 1) * 128] = o.astype(o_ref.dtype)


def swa_attention(z, bias_tab, sinks, B, S):
    T = z.shape[0]
    nb = S // WINDOW
    return pl.pallas_call(
        _swa_kernel,
        grid=(B, nb),
        in_specs=[pl.BlockSpec(memory_space=pltpu.SMEM),
                  pl.BlockSpec((WINDOW, 2048), lambda b, i: (b * nb + i, 0)),
                  pl.BlockSpec((WINDOW, 512), lambda b, i: (b * nb + i, 4)),
                  pl.BlockSpec((WINDOW, 512), lambda b, i: (b * nb + jnp.maximum(i - 1, 0), 4)),
                  pl.BlockSpec(bias_tab.shape, lambda b, i: (0, 0, 0))],
        out_specs=pl.BlockSpec((WINDOW, 2048), lambda b, i: (b * nb + i, 0)),
        out_shape=jax.ShapeDtypeStruct((T, 2048), BF16),
        compiler_params=_params("parallel", "arbitrary"),
        name="swa_attention",
    )(sinks, z, z, z, bias_tab)


def _decay_kernel(f_ref, b_ref, o_ref, carry_sc, *, tm):
    @pl.when(pl.program_id(1) == 0)
    def _():
        carry_sc[...] = jnp.zeros(carry_sc.shape, F32)

    x = f_ref[...] + b_ref[...]
    lf = jnp.minimum(x, 0.0) - jnp.log1p(jnp.exp(-jnp.abs(x)))
    row = lax.broadcasted_iota(jnp.int32, (tm, tm), 0)
    col = lax.broadcasted_iota(jnp.int32, (tm, tm), 1)
    tri = jnp.where(col <= row, 1.0, 0.0).astype(BF16)
    hi, mid, lo = _split3(lf)
    cs = _dot(tri, hi) + _dot(tri, mid) + _dot(tri, lo) + carry_sc[...]
    carry_sc[...] = cs[tm - 1:tm, :]
    o_ref[...] = cs.T


def decay_cumsum(f_logit, b_f, B, S, tm=512):
    tm = _tile(S, tm)
    ns = S // tm
    return pl.pallas_call(
        functools.partial(_decay_kernel, tm=tm),
        grid=(B, ns),
        in_specs=[pl.BlockSpec((tm, LANES), lambda b, i: (b * ns + i, 0)),
                  pl.BlockSpec((1, LANES), lambda b, i: (0, 0))],
        out_specs=pl.BlockSpec((None, LANES, tm), lambda b, i: (b, 0, i)),
        out_shape=jax.ShapeDtypeStruct((B, LANES, S), F32),
        scratch_shapes=[pltpu.VMEM((1, LANES), F32)],
        compiler_params=_params("parallel", "arbitrary"),
        name="decay_cumsum",
    )(f_logit, b_f)


def _cross_kernel(x_ref, g_ref, wq_ref, k_ref, v_ref, wo_ref, o_ref):
    x = x_ref[...]
    hn = _rms(x, g_ref[...]).astype(BF16)
    q = (_dot(hn, wq_ref[...]) * (X_HEAD_DIM ** -0.5)).astype(BF16)
    outs = []
    for h in range(X_HEADS):
        sl = slice(h * X_HEAD_DIM, (h + 1) * X_HEAD_DIM)
        s = _dot_nt(q[:, sl], k_ref[:, sl])
        m = jnp.max(s, axis=-1, keepdims=True)
        p = jnp.exp(s - m)
        o = _dot(p.astype(BF16), v_ref[:, sl]) / jnp.sum(p, axis=-1, keepdims=True)
        outs.append(o.astype(BF16))
    o_ref[...] = x + _dot(jnp.concatenate(outs, axis=1), wo_ref[...])


def cross_attention(x, g, wq, kv, wo, B, S, tm=512):
    T, D = x.shape
    M = kv.shape[1]
    XW = X_HEADS * X_HEAD_DIM
    tm = _tile(S, tm)
    ns = S // tm
    return pl.pallas_call(
        _cross_kernel,
        grid=(B, ns),
        in_specs=[pl.BlockSpec((tm, D), lambda b, i: (b * ns + i, 0)),
                  pl.BlockSpec((1, D), lambda b, i: (0, 0)),
                  pl.BlockSpec((D, XW), lambda b, i: (0, 0)),
                  pl.BlockSpec((None, M, XW), lambda b, i: (b, 0, 0)),
                  pl.BlockSpec((None, M, XW), lambda b, i: (b, 0, 1)),
                  pl.BlockSpec((XW, D), lambda b, i: (0, 0))],
        out_specs=pl.BlockSpec((tm, D), lambda b, i: (b * ns + i, 0)),
        out_shape=jax.ShapeDtypeStruct((T, D), F32),
        compiler_params=_params("parallel", "parallel"),
        name="cross_attention",
    )(x, g.reshape(1, D), wq, kv, kv, wo)


def _ffn_kernel(x_ref, g_ref, wg_ref, wu_ref, wd_ref, o_ref, h_sc):
    @pl.when(pl.program_id(1) == 0)
    def _():
        x = x_ref[...]
        h_sc[...] = _rms(x, g_ref[...]).astype(BF16)
        o_ref[...] = x

    h = h_sc[...]
    gate = _dot(h, wg_ref[...])
    up = _dot(h, wu_ref[...])
    a = (gate * jax.nn.sigmoid(gate) * up).astype(BF16)
    o_ref[...] += _dot(a, wd_ref[...])


def ffn(x, g, wg, wu, wd, tm=512, tf=512):
    T, D = x.shape
    F = wg.shape[1]
    tm, tf = _tile(T, tm), _tile(F, tf)
    return pl.pallas_call(
        _ffn_kernel,
        grid=(T // tm, F // tf),
        in_specs=[pl.BlockSpec((tm, D), lambda i, f: (i, 0)),
                  pl.BlockSpec((1, D), lambda i, f: (0, 0)),
                  pl.BlockSpec((D, tf), lambda i, f: (0, f)),
                  pl.BlockSpec((D, tf), lambda i, f: (0, f)),
                  pl.BlockSpec((tf, D), lambda i, f: (f, 0))],
        out_specs=pl.BlockSpec((tm, D), lambda i, f: (i, 0)),
        out_shape=jax.ShapeDtypeStruct((T, D), F32),
        scratch_shapes=[pltpu.VMEM((tm, D), BF16)],
        compiler_params=_params("parallel", "arbitrary"),
        name="ffn",
    )(x, g.reshape(1, D), wg, wu, wd)


def _router_kernel(x_ref, g_ref, wh_ref, wl_ref, b_ref, c_ref):
    hn = _rms(x_ref[...], g_ref[...])
    h_hi = hn.astype(BF16)
    h_lo = (hn - h_hi.astype(F32)).astype(BF16)
    logits = (_dot(h_hi, wh_ref[...]) + _dot(h_lo, wh_ref[...]) + _dot(h_hi, wl_ref[...])) + b_ref[...]
    lane = lax.broadcasted_iota(jnp.int32, logits.shape, 1)
    logits = jnp.where(lane < N_EXPERTS, logits, -jnp.inf)
    m1 = jnp.max(logits, axis=-1, keepdims=True)
    i1 = jnp.min(jnp.where(logits == m1, lane, LANES), axis=-1, keepdims=True)
    oh1 = lane == i1
    l2 = jnp.where(oh1, -jnp.inf, logits)
    m2 = jnp.max(l2, axis=-1, keepdims=True)
    i2 = jnp.min(jnp.where(l2 == m2, lane, LANES), axis=-1, keepdims=True)
    oh2 = lane == i2
    e2 = jnp.exp(m2 - m1)
    g1 = 1.0 / (1.0 + e2)
    c_ref[...] = jnp.where(oh1, g1, 0.0) + jnp.where(oh2, e2 * g1, 0.0)


def router(x, g, w_router, b_router, tm=512):
    T, D = x.shape
    tm = _tile(T, tm)
    wp = jnp.pad(w_router, ((0, 0), (0, LANES - N_EXPERTS)))
    w_hi = wp.astype(BF16)
    w_lo = (wp - w_hi.astype(F32)).astype(BF16)
    bp = jnp.pad(b_router, (0, LANES - N_EXPERTS)).reshape(1, LANES)
    return pl.pallas_call(
        _router_kernel,
        grid=(T // tm,),
        in_specs=[pl.BlockSpec((tm, D), lambda i: (i, 0)),
                  pl.BlockSpec((1, D), lambda i: (0, 0)),
                  pl.BlockSpec((D, LANES), lambda i: (0, 0)),
                  pl.BlockSpec((D, LANES), lambda i: (0, 0)),
                  pl.BlockSpec((1, LANES), lambda i: (0, 0))],
        out_specs=pl.BlockSpec((tm, LANES), lambda i: (i, 0)),
        out_shape=jax.ShapeDtypeStruct((T, LANES), F32),
        compiler_params=_params("parallel"),
        name="router",
    )(x, g.reshape(1, D), w_hi, w_lo, bp)


def _moe_dense_kernel(x_ref, g_ref, c_ref, wg_ref, wu_ref, wd_ref, o_ref, h_sc):
    e = pl.program_id(1)

    @pl.when((e == 0) & (pl.program_id(2) == 0))
    def _():
        x = x_ref[...]
        h_sc[...] = _rms(x, g_ref[...]).astype(BF16)
        o_ref[...] = x

    c = c_ref[...]
    lane = lax.broadcasted_iota(jnp.int32, c.shape, 1)
    ce = jnp.sum(jnp.where(lane == e, c, 0.0), axis=-1, keepdims=True)
    h = h_sc[...]
    gate = _dot(h, wg_ref[...])
    up = _dot(h, wu_ref[...])
    a = (gate * jax.nn.sigmoid(gate) * up * ce).astype(BF16)
    o_ref[...] += _dot(a, wd_ref[...])


def moe_dense(x, g, combine, wg, wu, wd, tm=512, tf=512):
    T, D = x.shape
    E, _, F = wg.shape
    tm, tf = _tile(T, tm), _tile(F, tf)
    return pl.pallas_call(
        _moe_dense_kernel,
        grid=(T // tm, E, F // tf),
        in_specs=[pl.BlockSpec((tm, D), lambda i, e, f: (i, 0)),
                  pl.BlockSpec((1, D), lambda i, e, f: (0, 0)),
                  pl.BlockSpec((tm, LANES), lambda i, e, f: (i, 0)),
                  pl.BlockSpec((None, D, tf), lambda i, e, f: (e, 0, f)),
                  pl.BlockSpec((None, D, tf), lambda i, e, f: (e, 0, f)),
                  pl.BlockSpec((None, tf, D), lambda i, e, f: (e, f, 0))],
        out_specs=pl.BlockSpec((tm, D), lambda i, e, f: (i, 0)),
        out_shape=jax.ShapeDtypeStruct((T, D), F32),
        scratch_shapes=[pltpu.VMEM((tm, D), BF16)],
        compiler_params=_params("parallel", "arbitrary", "arbitrary"),
        name="moe_dense",
    )(x, g.reshape(1, D), combine, wg, wu, wd)


def _final_norm_kernel(x_ref, g_ref, o_ref):
    o_ref[...] = _rms(x_ref[...], g_ref[...])


def final_norm(x, g, tm=512):
    T, D = x.shape
    tm = _tile(T, tm)
    return pl.pallas_call(
        _final_norm_kernel,
        grid=(T // tm,),
        in_specs=[pl.BlockSpec((tm, D), lambda i: (i, 0)), pl.BlockSpec((1, D), lambda i: (0, 0))],
        out_specs=pl.BlockSpec((tm, D), lambda i: (i, 0)),
        out_shape=jax.ShapeDtypeStruct((T, D), F32),
        compiler_params=_params("parallel"),
        name="final_norm",
    )(x, g.reshape(1, D))


def _pad_heads(w, heads, dim, to=LANES):
    lead = w.shape[:-1]
    w = w.reshape(lead + (heads, dim))
    w = jnp.pad(w, [(0, 0)] * len(lead) + [(0, 0), (0, to - dim)])
    return w.reshape(lead + (heads * to,))


def _swap_halves(w):
    half = w.shape[-1] // 2
    return jnp.concatenate([w[..., half:], w[..., :half]], axis=-1)


def _even_w_in(w):
    c_q = w[:, :MLA_Q_RANK]
    o = MLA_Q_RANK
    c_kv = w[:, o:o + MLA_KV_RANK]
    o += MLA_KV_RANK
    k_rope = w[:, o:o + MLA_ROPE]
    o += MLA_ROPE
    q_s = w[:, o:o + SWA_HEADS * SWA_HEAD_DIM]
    o += SWA_HEADS * SWA_HEAD_DIM
    k_s = w[:, o:o + SWA_KV_HEADS * SWA_HEAD_DIM]
    o += SWA_KV_HEADS * SWA_HEAD_DIM
    v_s = w[:, o:o + SWA_KV_HEADS * SWA_HEAD_DIM]
    pad = lambda a: jnp.pad(a, ((0, 0), (0, LANES - a.shape[1])))
    return jnp.concatenate([
        _pad_heads(q_s, SWA_HEADS, SWA_HEAD_DIM),
        _pad_heads(k_s, SWA_KV_HEADS, SWA_HEAD_DIM),
        _pad_heads(v_s, SWA_KV_HEADS, SWA_HEAD_DIM),
        c_q, c_kv, pad(k_rope), pad(_swap_halves(k_rope))], axis=1).astype(BF16)


def _t5_bucket(dist):
    max_exact = REL_BUCKETS // 2
    large = max_exact + (jnp.log(jnp.maximum(dist, 1).astype(F32) / max_exact)
                         / math.log(REL_MAX_DIST / max_exact) * (REL_BUCKETS - max_exact)).astype(jnp.int32)
    large = jnp.minimum(large, REL_BUCKETS - 1)
    return jnp.where(dist < max_exact, dist, large)


def _swa_bias_table(rel_bias):
    i = jnp.arange(WINDOW)[:, None]
    j = jnp.arange(2 * WINDOW)[None, :]
    rel = WINDOW + i - j
    bias = rel_bias[_t5_bucket(jnp.maximum(rel, 0))].astype(F32)
    valid = (rel >= 0) & (rel < WINDOW)
    return jnp.where(valid[None], jnp.transpose(bias, (2, 0, 1)), NEG)


def _rope_tables(positions):
    half = MLA_ROPE // 2
    inv = ROPE_THETA ** (-jnp.arange(half, dtype=F32) / half)
    ang = positions.astype(F32).reshape(-1, 1) * inv
    cos, sin = jnp.cos(ang), jnp.sin(ang)
    z = jnp.zeros((ang.shape[0], LANES - MLA_ROPE), F32)
    return jnp.concatenate([cos, cos, z], axis=1), jnp.concatenate([-sin, sin, z], axis=1)


def kernel(x, mem, positions, rel_bias, norm_mix, norm_cross, norm_mem, norm_ffn, x_wq, x_wk, x_wv, x_wo,
           ev_w_in, ev_mla_q_norm, ev_mla_w_uq, ev_mla_kv_norm, ev_mla_w_ukv, ev_swa_sinks, ev_w_o,
           ev_ffn_w_gate, ev_ffn_w_up, ev_ffn_w_down, od_w_in, od_fox_b_f, od_w_o, od_router_w, od_router_b,
           od_moe_w_gate, od_moe_w_up, od_moe_w_down, final_norm_g):
    B, S, D = x.shape
    T = B * S
    M = mem.shape[1]
    xs = x.reshape(T, D)
    mem2 = mem.reshape(B * M, D)
    bf = lambda a: a.astype(BF16)

    def cross(xs, layer):
        kvw = bf(jnp.concatenate([x_wk[layer], x_wv[layer]], axis=1))
        kv = norm_matmul(mem2, norm_mem[layer], kvw, tm=256, tn=1024).reshape(B, M, -1)
        return cross_attention(xs, norm_cross[layer], bf(x_wq[layer]), kv, bf(x_wo[layer]), B, S)

    z0 = norm_matmul(xs, norm_mix[0], _even_w_in(ev_w_in[0]), tm=512, tn=2048)
    cos2, sin2 = _rope_tables(positions)
    w_uq = ev_mla_w_uq[0].reshape(MLA_Q_RANK, MLA_HEADS, MLA_NOPE + MLA_ROPE)
    rope_w = w_uq[..., MLA_NOPE:]
    padr = lambda a: jnp.pad(a, ((0, 0), (0, 0), (0, LANES - MLA_ROPE)))
    wq_all = bf(jnp.concatenate([w_uq[..., :MLA_NOPE], padr(rope_w)], axis=-1).reshape(MLA_Q_RANK, -1))
    wq_swap = bf(padr(_swap_halves(rope_w)).reshape(MLA_Q_RANK, -1))
    w_ukv = ev_mla_w_ukv[0].reshape(MLA_KV_RANK, MLA_HEADS, MLA_NOPE + MLA_V)
    wk = bf(w_ukv[..., :MLA_NOPE].reshape(MLA_KV_RANK, -1))
    wv = bf(w_ukv[..., MLA_NOPE:].reshape(MLA_KV_RANK, -1))
    q, k, v = mla_prep(z0, cos2, sin2, ev_mla_q_norm[0], ev_mla_kv_norm[0], wq_all, wq_swap, wk, wv)
    o_mla = flash_attention(q.reshape(B, S, -1), k.reshape(B, S, -1), v.reshape(B, S, -1), None,
                            heads=MLA_HEADS, dk=256, dv=MLA_V, q_off=0, k_off=0, v_off=0)
    o_swa = swa_attention(z0, _swa_bias_table(rel_bias), ev_swa_sinks[0], B, S)
    n_mla = MLA_HEADS * MLA_V
    w_o_swa = ev_w_o[0][n_mla:].reshape(SWA_HEADS, SWA_HEAD_DIM, D)
    w_o_swa = jnp.pad(w_o_swa, ((0, 0), (0, LANES - SWA_HEAD_DIM), (0, 0))).reshape(SWA_HEADS * LANES, D)
    xs = matmul_res(xs, [(o_mla.reshape(T, n_mla), bf(ev_w_o[0][:n_mla])), (o_swa, bf(w_o_swa))])
    xs = cross(xs, 0)
    xs = ffn(xs, norm_ffn[0], bf(ev_ffn_w_gate[0]), bf(ev_ffn_w_up[0]), bf(ev_ffn_w_down[0]))

    FW = FOX_HEADS * FOX_HEAD_DIM
    col_scale = jnp.concatenate([jnp.full((1, FW), FOX_HEAD_DIM ** -0.5, F32), jnp.ones((1, 2 * FW), F32)], axis=1)
    w_f = jnp.pad(od_w_in[0][:, 3 * FW:], ((0, 0), (0, LANES - FOX_HEADS)))
    z1, f_logit = norm_matmul(xs, norm_mix[1], bf(od_w_in[0][:, :3 * FW]), col_scale=col_scale, side_w=w_f,
                              tm=512, tn=2048)
    b_f = jnp.pad(od_fox_b_f[0], (0, LANES - FOX_HEADS)).reshape(1, LANES)
    tq = _tile(S, 512)
    decay = decay_cumsum(f_logit, b_f, B, S)[:, :FOX_HEADS, :].reshape(B, FOX_HEADS, S // tq, tq)
    z1 = z1.reshape(B, S, 3 * FW)
    o_fox = flash_attention(z1, z1, z1, decay, heads=FOX_HEADS, dk=FOX_HEAD_DIM, dv=FOX_HEAD_DIM,
                            q_off=0, k_off=FOX_HEADS, v_off=2 * FOX_HEADS, tq=tq)
    xs = matmul_res(xs, [(o_fox.reshape(T, FW), bf(od_w_o[0]))])
    xs = cross(xs, 1)
    combine = router(xs, norm_ffn[1], od_router_w[0], od_router_b[0])
    xs = moe_dense(xs, norm_ffn[1], combine, bf(od_moe_w_gate[0]), bf(od_moe_w_up[0]), bf(od_moe_w_down[0]))
    return final_norm(xs, final_norm_g).reshape(B, S, D)
```

```python
import functools
import math

import jax
import jax.numpy as jnp
from jax import lax
from jax.experimental import pallas as pl
from jax.experimental.pallas import tpu as pltpu

F32 = jnp.float32
BF16 = jnp.bfloat16

EPS = 1e-6
NEG = -1e30
LANES = 128
LOG2E = math.log2(math.e)

MLA_HEADS = 8
MLA_Q_RANK = 768
MLA_KV_RANK = 512
MLA_NOPE = 128
MLA_ROPE = 64
MLA_V = 128
ROPE_THETA = 10000.0
SWA_HEADS = 16
SWA_KV_HEADS = 2
SWA_GROUP = SWA_HEADS // SWA_KV_HEADS
SWA_HEAD_DIM = 64
WINDOW = 128
REL_BUCKETS = 32
REL_MAX_DIST = 128
FOX_HEADS = 16
FOX_HEAD_DIM = 128
X_HEADS = 4
X_HEAD_DIM = 128
N_EXPERTS = 8

VMEM_LIMIT_BYTES = 56 * 1024 * 1024


def _tile(n, pref):
    return pref if n % pref == 0 else n


def _params(*sem):
    return pltpu.CompilerParams(dimension_semantics=sem, vmem_limit_bytes=VMEM_LIMIT_BYTES)


def _rms(xf, g):
    ms = jnp.mean(xf * xf, axis=-1, keepdims=True)
    return xf * lax.rsqrt(ms + EPS) * g


def _split3(v):
    hi = v.astype(BF16)
    r = v - hi.astype(F32)
    mid = r.astype(BF16)
    lo = (r - mid.astype(F32)).astype(BF16)
    return hi, mid, lo


def _dot(a, b):
    return jnp.dot(a, b, preferred_element_type=F32)


def _dot_nt(a, b):
    return lax.dot_general(a, b, (((1,), (1,)), ((), ())), preferred_element_type=F32)


def _norm_matmul_kernel(x_ref, g_ref, w_ref, cs_ref, *rest, side):
    if side:
        wsh_ref, wsl_ref, o_ref, so_ref, h_sc = rest
    else:
        o_ref, h_sc = rest

    @pl.when(pl.program_id(1) == 0)
    def _():
        hn = _rms(x_ref[...], g_ref[...])
        h_hi = hn.astype(BF16)
        h_sc[...] = h_hi
        if side:
            h_lo = (hn - h_hi.astype(F32)).astype(BF16)
            so_ref[...] = (_dot(h_hi, wsh_ref[...]) + _dot(h_lo, wsh_ref[...])
                           + _dot(h_hi, wsl_ref[...]))

    o_ref[...] = (_dot(h_sc[...], w_ref[...]) * cs_ref[...]).astype(o_ref.dtype)


def norm_matmul(x, g, w, col_scale=None, side_w=None, tm=512, tn=2048):
    T, D = x.shape
    N = w.shape[1]
    tm, tn = _tile(T, tm), _tile(N, tn)
    if col_scale is None:
        col_scale = jnp.ones((1, N), F32)
    in_specs = [pl.BlockSpec((tm, D), lambda i, j: (i, 0)),
                pl.BlockSpec((1, D), lambda i, j: (0, 0)),
                pl.BlockSpec((D, tn), lambda i, j: (0, j)),
                pl.BlockSpec((1, tn), lambda i, j: (0, j))]
    args = [x, g.reshape(1, D), w, col_scale]
    out_shape = [jax.ShapeDtypeStruct((T, N), BF16)]
    out_specs = [pl.BlockSpec((tm, tn), lambda i, j: (i, j))]
    side = side_w is not None
    if side:
        ws_hi = side_w.astype(BF16)
        ws_lo = (side_w - ws_hi.astype(F32)).astype(BF16)
        ns = side_w.shape[1]
        in_specs += [pl.BlockSpec((D, ns), lambda i, j: (0, 0))] * 2
        args += [ws_hi, ws_lo]
        out_shape.append(jax.ShapeDtypeStruct((T, ns), F32))
        out_specs.append(pl.BlockSpec((tm, ns), lambda i, j: (i, 0)))
    res = pl.pallas_call(
        functools.partial(_norm_matmul_kernel, side=side),
        grid=(T // tm, N // tn),
        in_specs=in_specs, out_specs=out_specs, out_shape=out_shape,
        scratch_shapes=[pltpu.VMEM((tm, D), BF16)],
        compiler_params=_params("parallel", "arbitrary"),
        name="norm_matmul",
    )(*args)
    return res if side else res[0]


def _matmul_res_kernel(*refs, n):
    x_ref = refs[0]
    o_ref = refs[1 + 2 * n]
    acc = x_ref[...]
    for k in range(n):
        acc = acc + _dot(refs[1 + 2 * k][...], refs[2 + 2 * k][...])
    o_ref[...] = acc


def matmul_res(x, pairs, tm=512, tn=1024):
    T, N = x.shape
    tm, tn = _tile(T, tm), _tile(N, tn)
    in_specs = [pl.BlockSpec((tm, tn), lambda i, j: (i, j))]
    args = [x]
    for a, w in pairs:
        K = a.shape[1]
        in_specs += [pl.BlockSpec((tm, K), lambda i, j: (i, 0)),
                     pl.BlockSpec((K, tn), lambda i, j: (0, j))]
        args += [a, w]
    return pl.pallas_call(
        functools.partial(_matmul_res_kernel, n=len(pairs)),
        grid=(T // tm, N // tn),
        in_specs=in_specs,
        out_specs=pl.BlockSpec((tm, tn), lambda i, j: (i, j)),
        out_shape=jax.ShapeDtypeStruct((T, N), F32),
        compiler_params=_params("parallel", "parallel"),
        name="matmul_res",
    )(*args)


def _mla_prep_kernel(z_ref, cos_ref, sin_ref, qg_ref, kvg_ref, wq_ref, wqs_ref, wk_ref, wv_ref,
                     q_ref, k_ref, v_ref, *, scale):
    cos2 = cos_ref[...]
    sin2 = sin_ref[...]
    cq = _rms(z_ref[:, 512:1280].astype(F32), qg_ref[...]).astype(BF16)
    qa = _dot(cq, wq_ref[...])
    qs = _dot(cq, wqs_ref[...])
    for h in range(MLA_HEADS):
        a, b = h * 256, h * 256 + 128
        q_ref[:, a:b] = (qa[:, a:b] * scale).astype(BF16)
        q_ref[:, b:b + 128] = ((qa[:, b:b + 128] * cos2 + qs[:, h * 128:(h + 1) * 128] * sin2)
                               * scale).astype(BF16)
    ckv = _rms(z_ref[:, 1280:1792].astype(F32), kvg_ref[...]).astype(BF16)
    kn = _dot(ckv, wk_ref[...])
    v_ref[...] = _dot(ckv, wv_ref[...]).astype(BF16)
    kpe = (z_ref[:, 1792:1920].astype(F32) * cos2 + z_ref[:, 1920:2048].astype(F32) * sin2).astype(BF16)
    for h in range(MLA_HEADS):
        k_ref[:, h * 256:h * 256 + 128] = kn[:, h * 128:(h + 1) * 128].astype(BF16)
        k_ref[:, h * 256 + 128:(h + 1) * 256] = kpe


def mla_prep(z, cos2, sin2, q_norm, kv_norm, wq_all, wq_swap, wk, wv, tm=512):
    T = z.shape[0]
    tm = _tile(T, tm)
    full = lambda a: pl.BlockSpec(a.shape, lambda i: (0, 0))
    qg, kvg = q_norm.reshape(1, -1), kv_norm.reshape(1, -1)
    return pl.pallas_call(
        functools.partial(_mla_prep_kernel, scale=LOG2E * (MLA_NOPE + MLA_ROPE) ** -0.5),
        grid=(T // tm,),
        in_specs=[pl.BlockSpec((tm, 2048), lambda i: (i, 1)),
                  pl.BlockSpec((tm, LANES), lambda i: (i, 0)),
                  pl.BlockSpec((tm, LANES), lambda i: (i, 0)),
                  full(qg), full(kvg), full(wq_all), full(wq_swap), full(wk), full(wv)],
        out_specs=[pl.BlockSpec((tm, MLA_HEADS * 256), lambda i: (i, 0)),
                   pl.BlockSpec((tm, MLA_HEADS * 256), lambda i: (i, 0)),
                   pl.BlockSpec((tm, MLA_HEADS * MLA_V), lambda i: (i, 0))],
        out_shape=[jax.ShapeDtypeStruct((T, MLA_HEADS * 256), BF16),
                   jax.ShapeDtypeStruct((T, MLA_HEADS * 256), BF16),
                   jax.ShapeDtypeStruct((T, MLA_HEADS * MLA_V), BF16)],
        compiler_params=_params("parallel"),
        name="mla_prep",
    )(z, cos2, sin2, qg, kvg, wq_all, wq_swap, wk, wv)


def _flash_kernel(*refs, tq, tk, groups, has_bias):
    if has_bias:
        q_ref, k_ref, v_ref, kb_ref, o_ref, m_sc, acc_sc, s_sc = refs
    else:
        q_ref, k_ref, v_ref, o_ref, m_sc, acc_sc, s_sc = refs
        kb_ref = None
    i = pl.program_id(2)
    rg = tq // groups
    dv = v_ref.shape[-1]
    m_sc[...] = jnp.full(m_sc.shape, NEG, F32)
    acc_sc[...] = jnp.zeros(acc_sc.shape, F32)
    ones = jnp.ones((tk, LANES), BF16)

    def scores(j, slot):
        k = k_ref[pl.ds(pl.multiple_of(j * tk, tk), tk), :]
        for g in range(groups):
            rows = slice(g * rg, (g + 1) * rg)
            s = _dot_nt(q_ref[rows, :], k)
            if has_bias:
                s = s - kb_ref[pl.ds(j, 1), :]
            s_sc[slot, rows, :] = s

    def consume(j, slot, masked):
        v1 = jnp.concatenate([v_ref[pl.ds(pl.multiple_of(j * tk, tk), tk), :], ones], axis=1)
        for g in range(groups):
            rows = slice(g * rg, (g + 1) * rg)
            s = s_sc[slot, rows, :]
            if masked:
                qpos = i * tq + g * rg + lax.broadcasted_iota(jnp.int32, (rg, tk), 0)
                kpos = j * tk + lax.broadcasted_iota(jnp.int32, (rg, tk), 1)
                s = jnp.where(kpos <= qpos, s, NEG)
            m_prev = m_sc[rows, :]
            m_new = jnp.maximum(m_prev, jnp.max(s, axis=-1, keepdims=True))
            alpha = jnp.exp2(m_prev - m_new)
            p = jnp.exp2((s - m_new).astype(BF16))
            acc_sc[rows, :] = alpha * acc_sc[rows, :] + _dot(p, v1)
            m_sc[rows, :] = m_new

    n_full = (i * tq) // tk
    scores(0, 0)

    def pair(jj, c):
        j = 2 * jj
        scores(j + 1, 1)
        consume(j, 0, False)
        scores(j + 2, 0)
        consume(j + 1, 1, False)
        return c

    lax.fori_loop(0, n_full // 2, pair, 0)
    r = 2 * (n_full // 2)

    @pl.when(n_full % 2 == 1)
    def _():
        scores(r + 1, 1)
        consume(r, 0, False)
        consume(r + 1, 1, True)

    @pl.when(n_full % 2 == 0)
    def _():
        consume(r, 0, True)

    o_ref[...] = (acc_sc[:, :dv] / acc_sc[:, dv:]).astype(o_ref.dtype)


def flash_attention(q_arr, k_arr, v_arr, kbias, *, heads, dk, dv, q_off, k_off, v_off, tq, tk, groups=2):
    B, S = q_arr.shape[:2]
    assert S % tk == 0 and tk % tq == 0 and tq % groups == 0
    has_bias = kbias is not None
    in_specs = [pl.BlockSpec((None, tq, dk), lambda b, h, i: (b, i, q_off + h)),
                pl.BlockSpec((None, S, dk), lambda b, h, i: (b, 0, k_off + h)),
                pl.BlockSpec((None, S, dv), lambda b, h, i: (b, 0, v_off + h))]
    args = [q_arr, k_arr, v_arr]
    if has_bias:
        in_specs.append(pl.BlockSpec((None, None, S // tk, tk), lambda b, h, i: (b, h, 0, 0)))
        args.append(kbias)
    return pl.pallas_call(
        functools.partial(_flash_kernel, tq=tq, tk=tk, groups=groups, has_bias=has_bias),
        grid=(B, heads, S // tq),
        in_specs=in_specs,
        out_specs=pl.BlockSpec((None, tq, dv), lambda b, h, i: (b, i, h)),
        out_shape=jax.ShapeDtypeStruct((B, S, heads * dv), BF16),
        scratch_shapes=[pltpu.VMEM((tq, 1), F32), pltpu.VMEM((tq, dv + LANES), F32),
                        pltpu.VMEM((2, tq, tk), F32)],
        compiler_params=_params("parallel", "parallel", "arbitrary"),
        name="flash_attention",
    )(*args)


def _attn_tiles(S):
    tk = _tile(S, 1024)
    return _tile(tk, 512), tk


def _swa_kernel(sink_ref, q_ref, kvc_ref, kvp_ref, bias_ref, o_ref):
    i = pl.program_id(1)
    lane = lax.broadcasted_iota(jnp.int32, (1, 2 * WINDOW), 1)
    pen = jnp.where((lane < WINDOW) & (i == 0), NEG, 0.0).astype(F32)
    for g in range(SWA_KV_HEADS):
        kcat = jnp.concatenate([kvp_ref[:, g * 128:(g + 1) * 128], kvc_ref[:, g * 128:(g + 1) * 128]], axis=0)
        vcat = jnp.concatenate([kvp_ref[:, 256 + g * 128:256 + (g + 1) * 128],
                                kvc_ref[:, 256 + g * 128:256 + (g + 1) * 128]], axis=0)
        for hh in range(SWA_GROUP):
            h = g * SWA_GROUP + hh
            s = _dot_nt(q_ref[:, h * 128:(h + 1) * 128], kcat) * (SWA_HEAD_DIM ** -0.5) + bias_ref[h] + pen
            sink = sink_ref[h]
            m = jnp.maximum(jnp.max(s, axis=-1, keepdims=True), sink)
            p = jnp.exp(s - m)
            denom = jnp.sum(p, axis=-1, keepdims=True) + jnp.exp(sink - m)
            o = _dot(p.astype(BF16), vcat) / denom
            o_ref[:, h * 128:(h + 1) * 128] = o.astype(o_ref.dtype)


def swa_attention(z, bias_tab, sinks, B, S):
    T = z.shape[0]
    nb = S // WINDOW
    return pl.pallas_call(
        _swa_kernel,
        grid=(B, nb),
        in_specs=[pl.BlockSpec(memory_space=pltpu.SMEM),
                  pl.BlockSpec((WINDOW, 2048), lambda b, i: (b * nb + i, 0)),
                  pl.BlockSpec((WINDOW, 512), lambda b, i: (b * nb + i, 4)),
                  pl.BlockSpec((WINDOW, 512), lambda b, i: (b * nb + jnp.maximum(i - 1, 0), 4)),
                  pl.BlockSpec(bias_tab.shape, lambda b, i: (0, 0, 0))],
        out_specs=pl.BlockSpec((WINDOW, 2048), lambda b, i: (b * nb + i, 0)),
        out_shape=jax.ShapeDtypeStruct((T, 2048), BF16),
        compiler_params=_params("parallel", "arbitrary"),
        name="swa_attention",
    )(sinks, z, z, z, bias_tab)


def _decay_kernel(f_ref, b_ref, o_ref, carry_sc, *, tm):
    @pl.when(pl.program_id(1) == 0)
    def _():
        carry_sc[...] = jnp.zeros(carry_sc.shape, F32)

    x = f_ref[...] + b_ref[...]
    lf = jnp.minimum(x, 0.0) - jnp.log1p(jnp.exp(-jnp.abs(x)))
    row = lax.broadcasted_iota(jnp.int32, (tm, tm), 0)
    col = lax.broadcasted_iota(jnp.int32, (tm, tm), 1)
    tri = jnp.where(col <= row, 1.0, 0.0).astype(BF16)
    hi, mid, lo = _split3(lf)
    cs = _dot(tri, hi) + _dot(tri, mid) + _dot(tri, lo) + carry_sc[...]
    carry_sc[...] = cs[tm - 1:tm, :]
    o_ref[...] = (cs * LOG2E).T


def decay_cumsum(f_logit, b_f, B, S, tm=512):
    tm = _tile(S, tm)
    ns = S // tm
    return pl.pallas_call(
        functools.partial(_decay_kernel, tm=tm),
        grid=(B, ns),
        in_specs=[pl.BlockSpec((tm, LANES), lambda b, i: (b * ns + i, 0)),
                  pl.BlockSpec((1, LANES), lambda b, i: (0, 0))],
        out_specs=pl.BlockSpec((None, LANES, tm), lambda b, i: (b, 0, i)),
        out_shape=jax.ShapeDtypeStruct((B, LANES, S), F32),
        scratch_shapes=[pltpu.VMEM((1, LANES), F32)],
        compiler_params=_params("parallel", "arbitrary"),
        name="decay_cumsum",
    )(f_logit, b_f)


def _cross_kernel(x_ref, g_ref, wq_ref, k_ref, v_ref, wo_ref, o_ref):
    x = x_ref[...]
    hn = _rms(x, g_ref[...]).astype(BF16)
    q = (_dot(hn, wq_ref[...]) * (X_HEAD_DIM ** -0.5)).astype(BF16)
    outs = []
    for h in range(X_HEADS):
        sl = slice(h * X_HEAD_DIM, (h + 1) * X_HEAD_DIM)
        s = _dot_nt(q[:, sl], k_ref[:, sl])
        m = jnp.max(s, axis=-1, keepdims=True)
        p = jnp.exp(s - m)
        o = _dot(p.astype(BF16), v_ref[:, sl]) / jnp.sum(p, axis=-1, keepdims=True)
        outs.append(o.astype(BF16))
    o_ref[...] = x + _dot(jnp.concatenate(outs, axis=1), wo_ref[...])


def cross_attention(x, g, wq, kv, wo, B, S, tm=512):
    T, D = x.shape
    M = kv.shape[1]
    XW = X_HEADS * X_HEAD_DIM
    tm = _tile(S, tm)
    ns = S // tm
    return pl.pallas_call(
        _cross_kernel,
        grid=(B, ns),
        in_specs=[pl.BlockSpec((tm, D), lambda b, i: (b * ns + i, 0)),
                  pl.BlockSpec((1, D), lambda b, i: (0, 0)),
                  pl.BlockSpec((D, XW), lambda b, i: (0, 0)),
                  pl.BlockSpec((None, M, XW), lambda b, i: (b, 0, 0)),
                  pl.BlockSpec((None, M, XW), lambda b, i: (b, 0, 1)),
                  pl.BlockSpec((XW, D), lambda b, i: (0, 0))],
        out_specs=pl.BlockSpec((tm, D), lambda b, i: (b * ns + i, 0)),
        out_shape=jax.ShapeDtypeStruct((T, D), F32),
        compiler_params=_params("parallel", "parallel"),
        name="cross_attention",
    )(x, g.reshape(1, D), wq, kv, kv, wo)


def _ffn_kernel(x_ref, g_ref, wg_ref, wu_ref, wd_ref, o_ref, h_sc):
    @pl.when(pl.program_id(1) == 0)
    def _():
        x = x_ref[...]
        h_sc[...] = _rms(x, g_ref[...]).astype(BF16)
        o_ref[...] = x

    h = h_sc[...]
    gate = _dot(h, wg_ref[...])
    up = _dot(h, wu_ref[...])
    a = (gate * jax.nn.sigmoid(gate) * up).astype(BF16)
    o_ref[...] += _dot(a, wd_ref[...])


def ffn(x, g, wg, wu, wd, tm=512, tf=512):
    T, D = x.shape
    F = wg.shape[1]
    tm, tf = _tile(T, tm), _tile(F, tf)
    return pl.pallas_call(
        _ffn_kernel,
        grid=(T // tm, F // tf),
        in_specs=[pl.BlockSpec((tm, D), lambda i, f: (i, 0)),
                  pl.BlockSpec((1, D), lambda i, f: (0, 0)),
                  pl.BlockSpec((D, tf), lambda i, f: (0, f)),
                  pl.BlockSpec((D, tf), lambda i, f: (0, f)),
                  pl.BlockSpec((tf, D), lambda i, f: (f, 0))],
        out_specs=pl.BlockSpec((tm, D), lambda i, f: (i, 0)),
        out_shape=jax.ShapeDtypeStruct((T, D), F32),
        scratch_shapes=[pltpu.VMEM((tm, D), BF16)],
        compiler_params=_params("parallel", "arbitrary"),
        name="ffn",
    )(x, g.reshape(1, D), wg, wu, wd)


def _router_kernel(x_ref, g_ref, wh_ref, wl_ref, b_ref, c_ref):
    hn = _rms(x_ref[...], g_ref[...])
    h_hi = hn.astype(BF16)
    h_lo = (hn - h_hi.astype(F32)).astype(BF16)
    logits = (_dot(h_hi, wh_ref[...]) + _dot(h_lo, wh_ref[...]) + _dot(h_hi, wl_ref[...])) + b_ref[...]
    lane = lax.broadcasted_iota(jnp.int32, logits.shape, 1)
    logits = jnp.where(lane < N_EXPERTS, logits, -jnp.inf)
    m1 = jnp.max(logits, axis=-1, keepdims=True)
    i1 = jnp.min(jnp.where(logits == m1, lane, LANES), axis=-1, keepdims=True)
    oh1 = lane == i1
    l2 = jnp.where(oh1, -jnp.inf, logits)
    m2 = jnp.max(l2, axis=-1, keepdims=True)
    i2 = jnp.min(jnp.where(l2 == m2, lane, LANES), axis=-1, keepdims=True)
    oh2 = lane == i2
    e2 = jnp.exp(m2 - m1)
    g1 = 1.0 / (1.0 + e2)
    c_ref[...] = jnp.where(oh1, g1, 0.0) + jnp.where(oh2, e2 * g1, 0.0)


def router(x, g, w_router, b_router, tm=512):
    T, D = x.shape
    tm = _tile(T, tm)
    wp = jnp.pad(w_router, ((0, 0), (0, LANES - N_EXPERTS)))
    w_hi = wp.astype(BF16)
    w_lo = (wp - w_hi.astype(F32)).astype(BF16)
    bp = jnp.pad(b_router, (0, LANES - N_EXPERTS)).reshape(1, LANES)
    return pl.pallas_call(
        _router_kernel,
        grid=(T // tm,),
        in_specs=[pl.BlockSpec((tm, D), lambda i: (i, 0)),
                  pl.BlockSpec((1, D), lambda i: (0, 0)),
                  pl.BlockSpec((D, LANES), lambda i: (0, 0)),
                  pl.BlockSpec((D, LANES), lambda i: (0, 0)),
                  pl.BlockSpec((1, LANES), lambda i: (0, 0))],
        out_specs=pl.BlockSpec((tm, LANES), lambda i: (i, 0)),
        out_shape=jax.ShapeDtypeStruct((T, LANES), F32),
        compiler_params=_params("parallel"),
        name="router",
    )(x, g.reshape(1, D), w_hi, w_lo, bp)


def _moe_dense_kernel(x_ref, g_ref, c_ref, wg_ref, wu_ref, wd_ref, o_ref, h_sc):
    e = pl.program_id(1)

    @pl.when((e == 0) & (pl.program_id(2) == 0))
    def _():
        x = x_ref[...]
        h_sc[...] = _rms(x, g_ref[...]).astype(BF16)
        o_ref[...] = x

    c = c_ref[...]
    lane = lax.broadcasted_iota(jnp.int32, c.shape, 1)
    ce = jnp.sum(jnp.where(lane == e, c, 0.0), axis=-1, keepdims=True)
    h = h_sc[...]
    gate = _dot(h, wg_ref[...])
    up = _dot(h, wu_ref[...])
    a = (gate * jax.nn.sigmoid(gate) * up * ce).astype(BF16)
    o_ref[...] += _dot(a, wd_ref[...])


def moe_dense(x, g, combine, wg, wu, wd, tm=512, tf=512):
    T, D = x.shape
    E, _, F = wg.shape
    tm, tf = _tile(T, tm), _tile(F, tf)
    return pl.pallas_call(
        _moe_dense_kernel,
        grid=(T // tm, E, F // tf),
        in_specs=[pl.BlockSpec((tm, D), lambda i, e, f: (i, 0)),
                  pl.BlockSpec((1, D), lambda i, e, f: (0, 0)),
                  pl.BlockSpec((tm, LANES), lambda i, e, f: (i, 0)),
                  pl.BlockSpec((None, D, tf), lambda i, e, f: (e, 0, f)),
                  pl.BlockSpec((None, D, tf), lambda i, e, f: (e, 0, f)),
                  pl.BlockSpec((None, tf, D), lambda i, e, f: (e, f, 0))],
        out_specs=pl.BlockSpec((tm, D), lambda i, e, f: (i, 0)),
        out_shape=jax.ShapeDtypeStruct((T, D), F32),
        scratch_shapes=[pltpu.VMEM((tm, D), BF16)],
        compiler_params=_params("parallel", "arbitrary", "arbitrary"),
        name="moe_dense",
    )(x, g.reshape(1, D), combine, wg, wu, wd)


def _final_norm_kernel(x_ref, g_ref, o_ref):
    o_ref[...] = _rms(x_ref[...], g_ref[...])


def final_norm(x, g, tm=512):
    T, D = x.shape
    tm = _tile(T, tm)
    return pl.pallas_call(
        _final_norm_kernel,
        grid=(T // tm,),
        in_specs=[pl.BlockSpec((tm, D), lambda i: (i, 0)), pl.BlockSpec((1, D), lambda i: (0, 0))],
        out_specs=pl.BlockSpec((tm, D), lambda i: (i, 0)),
        out_shape=jax.ShapeDtypeStruct((T, D), F32),
        compiler_params=_params("parallel"),
        name="final_norm",
    )(x, g.reshape(1, D))


def _pad_heads(w, heads, dim, to=LANES):
    lead = w.shape[:-1]
    w = w.reshape(lead + (heads, dim))
    w = jnp.pad(w, [(0, 0)] * len(lead) + [(0, 0), (0, to - dim)])
    return w.reshape(lead + (heads * to,))


def _swap_halves(w):
    half = w.shape[-1] // 2
    return jnp.concatenate([w[..., half:], w[..., :half]], axis=-1)


def _even_w_in(w):
    c_q = w[:, :MLA_Q_RANK]
    o = MLA_Q_RANK
    c_kv = w[:, o:o + MLA_KV_RANK]
    o += MLA_KV_RANK
    k_rope = w[:, o:o + MLA_ROPE]
    o += MLA_ROPE
    q_s = w[:, o:o + SWA_HEADS * SWA_HEAD_DIM]
    o += SWA_HEADS * SWA_HEAD_DIM
    k_s = w[:, o:o + SWA_KV_HEADS * SWA_HEAD_DIM]
    o += SWA_KV_HEADS * SWA_HEAD_DIM
    v_s = w[:, o:o + SWA_KV_HEADS * SWA_HEAD_DIM]
    pad = lambda a: jnp.pad(a, ((0, 0), (0, LANES - a.shape[1])))
    return jnp.concatenate([
        _pad_heads(q_s, SWA_HEADS, SWA_HEAD_DIM),
        _pad_heads(k_s, SWA_KV_HEADS, SWA_HEAD_DIM),
        _pad_heads(v_s, SWA_KV_HEADS, SWA_HEAD_DIM),
        c_q, c_kv, pad(k_rope), pad(_swap_halves(k_rope))], axis=1).astype(BF16)


def _t5_bucket(dist):
    max_exact = REL_BUCKETS // 2
    large = max_exact + (jnp.log(jnp.maximum(dist, 1).astype(F32) / max_exact)
                         / math.log(REL_MAX_DIST / max_exact) * (REL_BUCKETS - max_exact)).astype(jnp.int32)
    large = jnp.minimum(large, REL_BUCKETS - 1)
    return jnp.where(dist < max_exact, dist, large)


def _swa_bias_table(rel_bias):
    i = jnp.arange(WINDOW)[:, None]
    j = jnp.arange(2 * WINDOW)[None, :]
    rel = WINDOW + i - j
    bias = rel_bias[_t5_bucket(jnp.maximum(rel, 0))].astype(F32)
    valid = (rel >= 0) & (rel < WINDOW)
    return jnp.where(valid[None], jnp.transpose(bias, (2, 0, 1)), NEG)


def _rope_tables(positions):
    half = MLA_ROPE // 2
    inv = ROPE_THETA ** (-jnp.arange(half, dtype=F32) / half)
    ang = positions.astype(F32).reshape(-1, 1) * inv
    cos, sin = jnp.cos(ang), jnp.sin(ang)
    z = jnp.zeros((ang.shape[0], LANES - MLA_ROPE), F32)
    return jnp.concatenate([cos, cos, z], axis=1), jnp.concatenate([-sin, sin, z], axis=1)


def kernel(x, mem, positions, rel_bias, norm_mix, norm_cross, norm_mem, norm_ffn, x_wq, x_wk, x_wv, x_wo,
           ev_w_in, ev_mla_q_norm, ev_mla_w_uq, ev_mla_kv_norm, ev_mla_w_ukv, ev_swa_sinks, ev_w_o,
           ev_ffn_w_gate, ev_ffn_w_up, ev_ffn_w_down, od_w_in, od_fox_b_f, od_w_o, od_router_w, od_router_b,
           od_moe_w_gate, od_moe_w_up, od_moe_w_down, final_norm_g):
    B, S, D = x.shape
    T = B * S
    M = mem.shape[1]
    xs = x.reshape(T, D)
    mem2 = mem.reshape(B * M, D)
    bf = lambda a: a.astype(BF16)

    def cross(xs, layer):
        kvw = bf(jnp.concatenate([x_wk[layer], x_wv[layer]], axis=1))
        kv = norm_matmul(mem2, norm_mem[layer], kvw, tm=256, tn=1024).reshape(B, M, -1)
        return cross_attention(xs, norm_cross[layer], bf(x_wq[layer]), kv, bf(x_wo[layer]), B, S)

    z0 = norm_matmul(xs, norm_mix[0], _even_w_in(ev_w_in[0]), tm=512, tn=2048)
    cos2, sin2 = _rope_tables(positions)
    w_uq = ev_mla_w_uq[0].reshape(MLA_Q_RANK, MLA_HEADS, MLA_NOPE + MLA_ROPE)
    rope_w = w_uq[..., MLA_NOPE:]
    padr = lambda a: jnp.pad(a, ((0, 0), (0, 0), (0, LANES - MLA_ROPE)))
    wq_all = bf(jnp.concatenate([w_uq[..., :MLA_NOPE], padr(rope_w)], axis=-1).reshape(MLA_Q_RANK, -1))
    wq_swap = bf(padr(_swap_halves(rope_w)).reshape(MLA_Q_RANK, -1))
    w_ukv = ev_mla_w_ukv[0].reshape(MLA_KV_RANK, MLA_HEADS, MLA_NOPE + MLA_V)
    wk = bf(w_ukv[..., :MLA_NOPE].reshape(MLA_KV_RANK, -1))
    wv = bf(w_ukv[..., MLA_NOPE:].reshape(MLA_KV_RANK, -1))
    q, k, v = mla_prep(z0, cos2, sin2, ev_mla_q_norm[0], ev_mla_kv_norm[0], wq_all, wq_swap, wk, wv)
    tq, tk = _attn_tiles(S)
    o_mla = flash_attention(q.reshape(B, S, -1), k.reshape(B, S, -1), v.reshape(B, S, -1), None,
                            heads=MLA_HEADS, dk=256, dv=MLA_V, q_off=0, k_off=0, v_off=0, tq=tq, tk=tk)
    o_swa = swa_attention(z0, _swa_bias_table(rel_bias), ev_swa_sinks[0], B, S)
    n_mla = MLA_HEADS * MLA_V
    w_o_swa = ev_w_o[0][n_mla:].reshape(SWA_HEADS, SWA_HEAD_DIM, D)
    w_o_swa = jnp.pad(w_o_swa, ((0, 0), (0, LANES - SWA_HEAD_DIM), (0, 0))).reshape(SWA_HEADS * LANES, D)
    xs = matmul_res(xs, [(o_mla.reshape(T, n_mla), bf(ev_w_o[0][:n_mla])), (o_swa, bf(w_o_swa))])
    xs = cross(xs, 0)
    xs = ffn(xs, norm_ffn[0], bf(ev_ffn_w_gate[0]), bf(ev_ffn_w_up[0]), bf(ev_ffn_w_down[0]))

    FW = FOX_HEADS * FOX_HEAD_DIM
    col_scale = jnp.concatenate([jnp.full((1, FW), LOG2E * FOX_HEAD_DIM ** -0.5, F32), jnp.ones((1, 2 * FW), F32)], axis=1)
    w_f = jnp.pad(od_w_in[0][:, 3 * FW:], ((0, 0), (0, LANES - FOX_HEADS)))
    z1, f_logit = norm_matmul(xs, norm_mix[1], bf(od_w_in[0][:, :3 * FW]), col_scale=col_scale, side_w=w_f,
                              tm=512, tn=2048)
    b_f = jnp.pad(od_fox_b_f[0], (0, LANES - FOX_HEADS)).reshape(1, LANES)
    decay = decay_cumsum(f_logit, b_f, B, S)[:, :FOX_HEADS, :].reshape(B, FOX_HEADS, S // tk, tk)
    z1 = z1.reshape(B, S, 3 * FW)
    o_fox = flash_attention(z1, z1, z1, decay, heads=FOX_HEADS, dk=FOX_HEAD_DIM, dv=FOX_HEAD_DIM,
                            q_off=0, k_off=FOX_HEADS, v_off=2 * FOX_HEADS, tq=tq, tk=tk)
    xs = matmul_res(xs, [(o_fox.reshape(T, FW), bf(od_w_o[0]))])
    xs = cross(xs, 1)
    combine = router(xs, norm_ffn[1], od_router_w[0], od_router_b[0])
    xs = moe_dense(xs, norm_ffn[1], combine, bf(od_moe_w_gate[0]), bf(od_moe_w_up[0]), bf(od_moe_w_down[0]))
    return final_norm(xs, final_norm_g).reshape(B, S, D)
```

```python
import functools
import math

import jax
import jax.numpy as jnp
from jax import lax
from jax.experimental import pallas as pl
from jax.experimental.pallas import tpu as pltpu

F32 = jnp.float32
BF16 = jnp.bfloat16

EPS = 1e-6
NEG = -1e30
LANES = 128
LOG2E = math.log2(math.e)

MLA_HEADS = 8
MLA_Q_RANK = 768
MLA_KV_RANK = 512
MLA_NOPE = 128
MLA_ROPE = 64
MLA_V = 128
ROPE_THETA = 10000.0
SWA_HEADS = 16
SWA_KV_HEADS = 2
SWA_GROUP = SWA_HEADS // SWA_KV_HEADS
SWA_HEAD_DIM = 64
WINDOW = 128
REL_BUCKETS = 32
REL_MAX_DIST = 128
FOX_HEADS = 16
FOX_HEAD_DIM = 128
X_HEADS = 4
X_HEAD_DIM = 128
N_EXPERTS = 8

VMEM_LIMIT_BYTES = 56 * 1024 * 1024


def _tile(n, pref):
    return pref if n % pref == 0 else n


def _params(*sem):
    return pltpu.CompilerParams(dimension_semantics=sem, vmem_limit_bytes=VMEM_LIMIT_BYTES)


def _rms(xf, g):
    ms = jnp.mean(xf * xf, axis=-1, keepdims=True)
    return xf * lax.rsqrt(ms + EPS) * g


def _split3(v):
    hi = v.astype(BF16)
    r = v - hi.astype(F32)
    mid = r.astype(BF16)
    lo = (r - mid.astype(F32)).astype(BF16)
    return hi, mid, lo


def _dot(a, b):
    return jnp.dot(a, b, preferred_element_type=F32)


def _dot_nt(a, b):
    return lax.dot_general(a, b, (((1,), (1,)), ((), ())), preferred_element_type=F32)


def _norm_matmul_kernel(x_ref, g_ref, w_ref, cs_ref, *rest, side):
    if side:
        wsh_ref, wsl_ref, o_ref, so_ref, h_sc = rest
    else:
        o_ref, h_sc = rest

    @pl.when(pl.program_id(1) == 0)
    def _():
        hn = _rms(x_ref[...], g_ref[...])
        h_hi = hn.astype(BF16)
        h_sc[...] = h_hi
        if side:
            h_lo = (hn - h_hi.astype(F32)).astype(BF16)
            so_ref[...] = (_dot(h_hi, wsh_ref[...]) + _dot(h_lo, wsh_ref[...])
                           + _dot(h_hi, wsl_ref[...]))

    o_ref[...] = (_dot(h_sc[...], w_ref[...]) * cs_ref[...]).astype(o_ref.dtype)


def norm_matmul(x, g, w, col_scale=None, side_w=None, tm=512, tn=2048):
    T, D = x.shape
    N = w.shape[1]
    tm, tn = _tile(T, tm), _tile(N, tn)
    if col_scale is None:
        col_scale = jnp.ones((1, N), F32)
    in_specs = [pl.BlockSpec((tm, D), lambda i, j: (i, 0)),
                pl.BlockSpec((1, D), lambda i, j: (0, 0)),
                pl.BlockSpec((D, tn), lambda i, j: (0, j)),
                pl.BlockSpec((1, tn), lambda i, j: (0, j))]
    args = [x, g.reshape(1, D), w, col_scale]
    out_shape = [jax.ShapeDtypeStruct((T, N), BF16)]
    out_specs = [pl.BlockSpec((tm, tn), lambda i, j: (i, j))]
    side = side_w is not None
    if side:
        ws_hi = side_w.astype(BF16)
        ws_lo = (side_w - ws_hi.astype(F32)).astype(BF16)
        ns = side_w.shape[1]
        in_specs += [pl.BlockSpec((D, ns), lambda i, j: (0, 0))] * 2
        args += [ws_hi, ws_lo]
        out_shape.append(jax.ShapeDtypeStruct((T, ns), F32))
        out_specs.append(pl.BlockSpec((tm, ns), lambda i, j: (i, 0)))
    res = pl.pallas_call(
        functools.partial(_norm_matmul_kernel, side=side),
        grid=(T // tm, N // tn),
        in_specs=in_specs, out_specs=out_specs, out_shape=out_shape,
        scratch_shapes=[pltpu.VMEM((tm, D), BF16)],
        compiler_params=_params("parallel", "arbitrary"),
        name="norm_matmul",
    )(*args)
    return res if side else res[0]


def _matmul_res_kernel(*refs, n):
    x_ref = refs[0]
    o_ref = refs[1 + 2 * n]
    acc = x_ref[...]
    for k in range(n):
        acc = acc + _dot(refs[1 + 2 * k][...], refs[2 + 2 * k][...])
    o_ref[...] = acc


def matmul_res(x, pairs, tm=512, tn=1024):
    T, N = x.shape
    tm, tn = _tile(T, tm), _tile(N, tn)
    in_specs = [pl.BlockSpec((tm, tn), lambda i, j: (i, j))]
    args = [x]
    for a, w in pairs:
        K = a.shape[1]
        in_specs += [pl.BlockSpec((tm, K), lambda i, j: (i, 0)),
                     pl.BlockSpec((K, tn), lambda i, j: (0, j))]
        args += [a, w]
    return pl.pallas_call(
        functools.partial(_matmul_res_kernel, n=len(pairs)),
        grid=(T // tm, N // tn),
        in_specs=in_specs,
        out_specs=pl.BlockSpec((tm, tn), lambda i, j: (i, j)),
        out_shape=jax.ShapeDtypeStruct((T, N), F32),
        compiler_params=_params("parallel", "parallel"),
        name="matmul_res",
    )(*args)


def _mla_prep_kernel(z_ref, cos_ref, sin_ref, qg_ref, kvg_ref, wq_ref, wqs_ref, wk_ref, wv_ref,
                     q_ref, k_ref, v_ref, *, scale):
    cos2 = cos_ref[...]
    sin2 = sin_ref[...]
    cq = _rms(z_ref[:, 512:1280].astype(F32), qg_ref[...]).astype(BF16)
    qa = _dot(cq, wq_ref[...])
    qs = _dot(cq, wqs_ref[...])
    for h in range(MLA_HEADS):
        a, b = h * 256, h * 256 + 128
        q_ref[:, a:b] = (qa[:, a:b] * scale).astype(BF16)
        q_ref[:, b:b + 128] = ((qa[:, b:b + 128] * cos2 + qs[:, h * 128:(h + 1) * 128] * sin2)
                               * scale).astype(BF16)
    ckv = _rms(z_ref[:, 1280:1792].astype(F32), kvg_ref[...]).astype(BF16)
    kn = _dot(ckv, wk_ref[...])
    v_ref[...] = _dot(ckv, wv_ref[...]).astype(BF16)
    kpe = (z_ref[:, 1792:1920].astype(F32) * cos2 + z_ref[:, 1920:2048].astype(F32) * sin2).astype(BF16)
    for h in range(MLA_HEADS):
        k_ref[:, h * 256:h * 256 + 128] = kn[:, h * 128:(h + 1) * 128].astype(BF16)
        k_ref[:, h * 256 + 128:(h + 1) * 256] = kpe


def mla_prep(z, cos2, sin2, q_norm, kv_norm, wq_all, wq_swap, wk, wv, tm=512):
    T = z.shape[0]
    tm = _tile(T, tm)
    full = lambda a: pl.BlockSpec(a.shape, lambda i: (0, 0))
    qg, kvg = q_norm.reshape(1, -1), kv_norm.reshape(1, -1)
    return pl.pallas_call(
        functools.partial(_mla_prep_kernel, scale=LOG2E * (MLA_NOPE + MLA_ROPE) ** -0.5),
        grid=(T // tm,),
        in_specs=[pl.BlockSpec((tm, 2048), lambda i: (i, 1)),
                  pl.BlockSpec((tm, LANES), lambda i: (i, 0)),
                  pl.BlockSpec((tm, LANES), lambda i: (i, 0)),
                  full(qg), full(kvg), full(wq_all), full(wq_swap), full(wk), full(wv)],
        out_specs=[pl.BlockSpec((tm, MLA_HEADS * 256), lambda i: (i, 0)),
                   pl.BlockSpec((tm, MLA_HEADS * 256), lambda i: (i, 0)),
                   pl.BlockSpec((tm, MLA_HEADS * MLA_V), lambda i: (i, 0))],
        out_shape=[jax.ShapeDtypeStruct((T, MLA_HEADS * 256), BF16),
                   jax.ShapeDtypeStruct((T, MLA_HEADS * 256), BF16),
                   jax.ShapeDtypeStruct((T, MLA_HEADS * MLA_V), BF16)],
        compiler_params=_params("parallel"),
        name="mla_prep",
    )(z, cos2, sin2, qg, kvg, wq_all, wq_swap, wk, wv)


def _flash_kernel(*refs, tq, tk, groups, has_bias):
    if has_bias:
        q_ref, k_ref, v_ref, kb_ref, o_ref, m_sc, acc_sc, s_sc = refs
    else:
        q_ref, k_ref, v_ref, o_ref, m_sc, acc_sc, s_sc = refs
        kb_ref = None
    i = pl.program_id(2)
    rg = tq // groups
    dv = v_ref.shape[-1]
    m_sc[...] = jnp.full(m_sc.shape, NEG, F32)
    acc_sc[...] = jnp.zeros(acc_sc.shape, F32)
    ones = jnp.ones((tk, LANES), BF16)

    def scores(j, slot):
        k = k_ref[pl.ds(pl.multiple_of(j * tk, tk), tk), :]
        for g in range(groups):
            rows = slice(g * rg, (g + 1) * rg)
            s = _dot_nt(q_ref[rows, :], k)
            if has_bias:
                s = s - kb_ref[pl.ds(j, 1), :]
            s_sc[slot, rows, :] = s

    def consume(j, slot, masked):
        v1 = jnp.concatenate([v_ref[pl.ds(pl.multiple_of(j * tk, tk), tk), :], ones], axis=1)
        for g in range(groups):
            rows = slice(g * rg, (g + 1) * rg)
            s = s_sc[slot, rows, :]
            if masked:
                qpos = i * tq + g * rg + lax.broadcasted_iota(jnp.int32, (rg, tk), 0)
                kpos = j * tk + lax.broadcasted_iota(jnp.int32, (rg, tk), 1)
                s = jnp.where(kpos <= qpos, s, NEG)
            m_prev = m_sc[rows, :]
            m_new = jnp.maximum(m_prev, jnp.max(s, axis=-1, keepdims=True))
            alpha = jnp.exp2(m_prev - m_new)
            p = jnp.exp2((s - m_new).astype(BF16))
            acc_sc[rows, :] = alpha * acc_sc[rows, :] + _dot(p, v1)
            m_sc[rows, :] = m_new

    n_full = (i * tq) // tk
    scores(0, 0)

    def pair(jj, c):
        j = 2 * jj
        scores(j + 1, 1)
        consume(j, 0, False)
        scores(j + 2, 0)
        consume(j + 1, 1, False)
        return c

    lax.fori_loop(0, n_full // 2, pair, 0)
    r = 2 * (n_full // 2)

    @pl.when(n_full % 2 == 1)
    def _():
        scores(r + 1, 1)
        consume(r, 0, False)
        consume(r + 1, 1, True)

    @pl.when(n_full % 2 == 0)
    def _():
        consume(r, 0, True)

    o_ref[...] = (acc_sc[:, :dv] / acc_sc[:, dv:]).astype(o_ref.dtype)


def flash_attention(q_arr, k_arr, v_arr, kbias, *, heads, dk, dv, q_off, k_off, v_off, tq, tk, groups=2):
    B, S = q_arr.shape[:2]
    assert S % tk == 0 and tk % tq == 0 and tq % groups == 0
    has_bias = kbias is not None
    in_specs = [pl.BlockSpec((None, tq, dk), lambda b, h, i: (b, i, q_off + h)),
                pl.BlockSpec((None, S, dk), lambda b, h, i: (b, 0, k_off + h)),
                pl.BlockSpec((None, S, dv), lambda b, h, i: (b, 0, v_off + h))]
    args = [q_arr, k_arr, v_arr]
    if has_bias:
        in_specs.append(pl.BlockSpec((None, None, S // tk, tk), lambda b, h, i: (b, h, 0, 0)))
        args.append(kbias)
    return pl.pallas_call(
        functools.partial(_flash_kernel, tq=tq, tk=tk, groups=groups, has_bias=has_bias),
        grid=(B, heads, S // tq),
        in_specs=in_specs,
        out_specs=pl.BlockSpec((None, tq, dv), lambda b, h, i: (b, i, h)),
        out_shape=jax.ShapeDtypeStruct((B, S, heads * dv), BF16),
        scratch_shapes=[pltpu.VMEM((tq, 1), F32), pltpu.VMEM((tq, dv + LANES), F32),
                        pltpu.VMEM((2, tq, tk), F32)],
        compiler_params=_params("parallel", "parallel", "arbitrary"),
        name="flash_attention",
    )(*args)


def _attn_tiles(S):
    tk = _tile(S, 1024)
    return _tile(tk, 512), tk


def _swa_kernel(sink_ref, q_ref, kvc_ref, kvp_ref, bias_ref, o_ref):
    i = pl.program_id(1)
    lane = lax.broadcasted_iota(jnp.int32, (1, 2 * WINDOW), 1)
    pen = jnp.where((lane < WINDOW) & (i == 0), NEG, 0.0).astype(F32)
    for g in range(SWA_KV_HEADS):
        kcat = jnp.concatenate([kvp_ref[:, g * 128:(g + 1) * 128], kvc_ref[:, g * 128:(g + 1) * 128]], axis=0)
        vcat = jnp.concatenate([kvp_ref[:, 256 + g * 128:256 + (g + 1) * 128],
                                kvc_ref[:, 256 + g * 128:256 + (g + 1) * 128]], axis=0)
        for hh in range(SWA_GROUP):
            h = g * SWA_GROUP + hh
            s = _dot_nt(q_ref[:, h * 128:(h + 1) * 128], kcat) * (SWA_HEAD_DIM ** -0.5) + bias_ref[h] + pen
            sink = sink_ref[h]
            m = jnp.maximum(jnp.max(s, axis=-1, keepdims=True), sink)
            p = jnp.exp(s - m)
            denom = jnp.sum(p, axis=-1, keepdims=True) + jnp.exp(sink - m)
            o = _dot(p.astype(BF16), vcat) / denom
            o_ref[:, h * 128:(h + 1) * 128] = o.astype(o_ref.dtype)


def swa_attention(z, bias_tab, sinks, B, S):
    T = z.shape[0]
    nb = S // WINDOW
    return pl.pallas_call(
        _swa_kernel,
        grid=(B, nb),
        in_specs=[pl.BlockSpec(memory_space=pltpu.SMEM),
                  pl.BlockSpec((WINDOW, 2048), lambda b, i: (b * nb + i, 0)),
                  pl.BlockSpec((WINDOW, 512), lambda b, i: (b * nb + i, 4)),
                  pl.BlockSpec((WINDOW, 512), lambda b, i: (b * nb + jnp.maximum(i - 1, 0), 4)),
                  pl.BlockSpec(bias_tab.shape, lambda b, i: (0, 0, 0))],
        out_specs=pl.BlockSpec((WINDOW, 2048), lambda b, i: (b * nb + i, 0)),
        out_shape=jax.ShapeDtypeStruct((T, 2048), BF16),
        compiler_params=_params("parallel", "arbitrary"),
        name="swa_attention",
    )(sinks, z, z, z, bias_tab)


def _decay_kernel(f_ref, b_ref, o_ref, carry_sc, *, tm):
    @pl.when(pl.program_id(1) == 0)
    def _():
        carry_sc[...] = jnp.zeros(carry_sc.shape, F32)

    x = f_ref[...] + b_ref[...]
    lf = jnp.minimum(x, 0.0) - jnp.log1p(jnp.exp(-jnp.abs(x)))
    row = lax.broadcasted_iota(jnp.int32, (tm, tm), 0)
    col = lax.broadcasted_iota(jnp.int32, (tm, tm), 1)
    tri = jnp.where(col <= row, 1.0, 0.0).astype(BF16)
    hi, mid, lo = _split3(lf)
    cs = _dot(tri, hi) + _dot(tri, mid) + _dot(tri, lo) + carry_sc[...]
    carry_sc[...] = cs[tm - 1:tm, :]
    o_ref[...] = (cs * LOG2E).T


def decay_cumsum(f_logit, b_f, B, S, tm=512):
    tm = _tile(S, tm)
    ns = S // tm
    return pl.pallas_call(
        functools.partial(_decay_kernel, tm=tm),
        grid=(B, ns),
        in_specs=[pl.BlockSpec((tm, LANES), lambda b, i: (b * ns + i, 0)),
                  pl.BlockSpec((1, LANES), lambda b, i: (0, 0))],
        out_specs=pl.BlockSpec((None, LANES, tm), lambda b, i: (b, 0, i)),
        out_shape=jax.ShapeDtypeStruct((B, LANES, S), F32),
        scratch_shapes=[pltpu.VMEM((1, LANES), F32)],
        compiler_params=_params("parallel", "arbitrary"),
        name="decay_cumsum",
    )(f_logit, b_f)


def _cross_kernel(x_ref, g_ref, wq_ref, k_ref, v_ref, wo_ref, o_ref):
    x = x_ref[...]
    hn = _rms(x, g_ref[...]).astype(BF16)
    q = (_dot(hn, wq_ref[...]) * (X_HEAD_DIM ** -0.5)).astype(BF16)
    outs = []
    for h in range(X_HEADS):
        sl = slice(h * X_HEAD_DIM, (h + 1) * X_HEAD_DIM)
        s = _dot_nt(q[:, sl], k_ref[:, sl])
        m = jnp.max(s, axis=-1, keepdims=True)
        p = jnp.exp(s - m)
        o = _dot(p.astype(BF16), v_ref[:, sl]) / jnp.sum(p, axis=-1, keepdims=True)
        outs.append(o.astype(BF16))
    o_ref[...] = x + _dot(jnp.concatenate(outs, axis=1), wo_ref[...])


def cross_attention(x, g, wq, kv, wo, B, S, tm=512):
    T, D = x.shape
    M = kv.shape[1]
    XW = X_HEADS * X_HEAD_DIM
    tm = _tile(S, tm)
    ns = S // tm
    return pl.pallas_call(
        _cross_kernel,
        grid=(B, ns),
        in_specs=[pl.BlockSpec((tm, D), lambda b, i: (b * ns + i, 0)),
                  pl.BlockSpec((1, D), lambda b, i: (0, 0)),
                  pl.BlockSpec((D, XW), lambda b, i: (0, 0)),
                  pl.BlockSpec((None, M, XW), lambda b, i: (b, 0, 0)),
                  pl.BlockSpec((None, M, XW), lambda b, i: (b, 0, 1)),
                  pl.BlockSpec((XW, D), lambda b, i: (0, 0))],
        out_specs=pl.BlockSpec((tm, D), lambda b, i: (b * ns + i, 0)),
        out_shape=jax.ShapeDtypeStruct((T, D), F32),
        compiler_params=_params("parallel", "parallel"),
        name="cross_attention",
    )(x, g.reshape(1, D), wq, kv, kv, wo)


def _ffn_kernel(x_ref, g_ref, wg_ref, wu_ref, wd_ref, o_ref, h_sc):
    @pl.when(pl.program_id(1) == 0)
    def _():
        x = x_ref[...]
        h_sc[...] = _rms(x, g_ref[...]).astype(BF16)
        o_ref[...] = x

    h = h_sc[...]
    gate = _dot(h, wg_ref[...])
    up = _dot(h, wu_ref[...])
    a = (gate * jax.nn.sigmoid(gate) * up).astype(BF16)
    o_ref[...] += _dot(a, wd_ref[...])


def ffn(x, g, wg, wu, wd, tm=512, tf=512):
    T, D = x.shape
    F = wg.shape[1]
    tm, tf = _tile(T, tm), _tile(F, tf)
    return pl.pallas_call(
        _ffn_kernel,
        grid=(T // tm, F // tf),
        in_specs=[pl.BlockSpec((tm, D), lambda i, f: (i, 0)),
                  pl.BlockSpec((1, D), lambda i, f: (0, 0)),
                  pl.BlockSpec((D, tf), lambda i, f: (0, f)),
                  pl.BlockSpec((D, tf), lambda i, f: (0, f)),
                  pl.BlockSpec((tf, D), lambda i, f: (f, 0))],
        out_specs=pl.BlockSpec((tm, D), lambda i, f: (i, 0)),
        out_shape=jax.ShapeDtypeStruct((T, D), F32),
        scratch_shapes=[pltpu.VMEM((tm, D), BF16)],
        compiler_params=_params("parallel", "arbitrary"),
        name="ffn",
    )(x, g.reshape(1, D), wg, wu, wd)


def _router_kernel(x_ref, g_ref, wh_ref, wl_ref, b_ref, mi_ref, mg_ref, cnt_ref, carry_sc, *, tm):
    @pl.when(pl.program_id(0) == 0)
    def _():
        carry_sc[...] = jnp.zeros(carry_sc.shape, F32)

    hn = _rms(x_ref[...], g_ref[...])
    h_hi = hn.astype(BF16)
    h_lo = (hn - h_hi.astype(F32)).astype(BF16)
    logits = (_dot(h_hi, wh_ref[...]) + _dot(h_lo, wh_ref[...]) + _dot(h_hi, wl_ref[...])) + b_ref[...]
    lane = lax.broadcasted_iota(jnp.int32, logits.shape, 1)
    logits = jnp.where(lane < N_EXPERTS, logits, -jnp.inf)
    m1 = jnp.max(logits, axis=-1, keepdims=True)
    i1 = jnp.min(jnp.where(logits == m1, lane, LANES), axis=-1, keepdims=True)
    oh1 = lane == i1
    l2 = jnp.where(oh1, -jnp.inf, logits)
    m2 = jnp.max(l2, axis=-1, keepdims=True)
    i2 = jnp.min(jnp.where(l2 == m2, lane, LANES), axis=-1, keepdims=True)
    oh2 = lane == i2
    e2 = jnp.exp(m2 - m1)
    g1 = 1.0 / (1.0 + e2)
    mg_ref[...] = jnp.where(lane == 0, g1, jnp.where(lane == 1, e2 * g1, 0.0))
    row = lax.broadcasted_iota(jnp.int32, (tm, tm), 0)
    col = lax.broadcasted_iota(jnp.int32, (tm, tm), 1)
    tri = jnp.where(col < row, 1.0, 0.0).astype(BF16)
    oh1f = jnp.where(oh1, 1.0, 0.0)
    oh2f = jnp.where(oh2, 1.0, 0.0)
    ex1 = _dot(tri, oh1f.astype(BF16))
    ex2 = _dot(tri, oh2f.astype(BF16))
    tot1 = jnp.sum(oh1f, axis=0, keepdims=True)
    tot2 = jnp.sum(oh2f, axis=0, keepdims=True)
    carry = carry_sc[...]
    r1 = jnp.sum(jnp.where(oh1, ex1 + carry, 0.0), axis=-1, keepdims=True).astype(jnp.int32)
    r2 = jnp.sum(jnp.where(oh2, ex2 + (carry + tot1), 0.0), axis=-1, keepdims=True).astype(jnp.int32)
    mi_ref[...] = jnp.where(lane == 0, i1, jnp.where(lane == 1, i2, jnp.where(lane == 2, r1,
                            jnp.where(lane == 3, r2, 0))))
    total = carry + tot1 + tot2
    carry_sc[...] = total
    cnt_ref[...] = total


def router(x, g, w_router, b_router, tm=512):
    T, D = x.shape
    tm = _tile(T, tm)
    wp = jnp.pad(w_router, ((0, 0), (0, LANES - N_EXPERTS)))
    w_hi = wp.astype(BF16)
    w_lo = (wp - w_hi.astype(F32)).astype(BF16)
    bp = jnp.pad(b_router, (0, LANES - N_EXPERTS)).reshape(1, LANES)
    return pl.pallas_call(
        functools.partial(_router_kernel, tm=tm),
        grid=(T // tm,),
        in_specs=[pl.BlockSpec((tm, D), lambda i: (i, 0)),
                  pl.BlockSpec((1, D), lambda i: (0, 0)),
                  pl.BlockSpec((D, LANES), lambda i: (0, 0)),
                  pl.BlockSpec((D, LANES), lambda i: (0, 0)),
                  pl.BlockSpec((1, LANES), lambda i: (0, 0))],
        out_specs=[pl.BlockSpec((tm, LANES), lambda i: (i, 0)),
                   pl.BlockSpec((tm, LANES), lambda i: (i, 0)),
                   pl.BlockSpec((1, LANES), lambda i: (0, 0))],
        out_shape=[jax.ShapeDtypeStruct((T, LANES), jnp.int32),
                   jax.ShapeDtypeStruct((T, LANES), F32),
                   jax.ShapeDtypeStruct((1, LANES), F32)],
        scratch_shapes=[pltpu.VMEM((1, LANES), F32)],
        compiler_params=_params("arbitrary"),
        name="router",
    )(x, g.reshape(1, D), w_hi, w_lo, bp)


def _row_copy(src_ref, src_row, dst_ref, dst_row, sem):
    return pltpu.make_async_copy(src_ref.at[pl.ds(src_row, 1), :], dst_ref.at[pl.ds(dst_row, 1), :], sem)


def _dispatch_kernel(grp_ref, pos_ref, x_ref, xs_ref, zero_sc, sem, *, tm):
    def issue(r, c):
        _row_copy(x_ref, r, xs_ref, pos_ref[0, 2 * r], sem).start()
        _row_copy(x_ref, r, xs_ref, pos_ref[0, 2 * r + 1], sem).start()
        return c

    lax.fori_loop(0, tm, issue, 0, unroll=8)

    def pad_rows(fn):
        for e in range(N_EXPERTS):
            cnt, padded, off = grp_ref[e], grp_ref[N_EXPERTS + e], grp_ref[2 * N_EXPERTS + e]
            lax.fori_loop(cnt, padded, lambda r, c: fn(off + r, c), 0)
        lax.fori_loop(grp_ref[3 * N_EXPERTS], xs_ref.shape[0], fn, 0)

    @pl.when(pl.program_id(0) == 0)
    def _():
        zero_sc[...] = jnp.zeros(zero_sc.shape, F32)

        def start(p, c):
            _row_copy(zero_sc, 0, xs_ref, p, sem).start()
            return c

        def wait(p, c):
            _row_copy(zero_sc, 0, xs_ref, p, sem).wait()
            return c

        pad_rows(start)
        pad_rows(wait)

    def drain(r, c):
        _row_copy(x_ref, r, xs_ref, pos_ref[0, 2 * r], sem).wait()
        _row_copy(x_ref, r, xs_ref, pos_ref[0, 2 * r + 1], sem).wait()
        return c

    lax.fori_loop(0, tm, drain, 0, unroll=8)


def moe_dispatch(x, pos, grp, n_rows, tm=256):
    T, D = x.shape
    tm = _tile(T, tm)
    pos3 = pos.reshape(T // tm, 1, 2 * tm)
    return pl.pallas_call(
        functools.partial(_dispatch_kernel, tm=tm),
        grid_spec=pltpu.PrefetchScalarGridSpec(
            num_scalar_prefetch=1,
            grid=(T // tm,),
            in_specs=[pl.BlockSpec((None, 1, 2 * tm), lambda i, grp: (i, 0, 0), memory_space=pltpu.SMEM),
                      pl.BlockSpec((tm, D), lambda i, grp: (i, 0))],
            out_specs=pl.BlockSpec(memory_space=pl.ANY),
            scratch_shapes=[pltpu.VMEM((8, D), F32), pltpu.SemaphoreType.DMA(())]),
        out_shape=jax.ShapeDtypeStruct((n_rows, D), F32),
        compiler_params=_params("arbitrary"),
        name="moe_dispatch",
    )(grp, pos3, x)


def _moe_expert_kernel(te_ref, nact_ref, xs_ref, g_ref, wg_ref, wu_ref, wd_ref, y_ref, h_sc):
    i, f = pl.program_id(0), pl.program_id(1)
    active = i < nact_ref[0]

    @pl.when(active & (f == 0))
    def _():
        h_sc[...] = _rms(xs_ref[...], g_ref[...]).astype(BF16)

    @pl.when(active)
    def _():
        h = h_sc[...]
        gate = _dot(h, wg_ref[...])
        up = _dot(h, wu_ref[...])
        a = (gate * jax.nn.sigmoid(gate) * up).astype(BF16)
        contrib = _dot(a, wd_ref[...])

        @pl.when(f == 0)
        def _():
            y_ref[...] = contrib

        @pl.when(f != 0)
        def _():
            y_ref[...] += contrib

    @pl.when(jnp.logical_not(active) & (f == 0))
    def _():
        y_ref[...] = jnp.zeros(y_ref.shape, F32)


def moe_experts(xs, g, tile_expert, n_active, wg, wu, wd, tm, tf=512):
    P, D = xs.shape
    F = wg.shape[2]
    tf = _tile(F, tf)
    nf = F // tf
    row = lambda i, f, te, na: (jnp.minimum(i, na[0] - 1), 0)
    fcol = lambda i, f, na: jnp.where(i < na[0], f, nf - 1)
    return pl.pallas_call(
        _moe_expert_kernel,
        grid_spec=pltpu.PrefetchScalarGridSpec(
            num_scalar_prefetch=2,
            grid=(P // tm, nf),
            in_specs=[pl.BlockSpec((tm, D), row),
                      pl.BlockSpec((1, D), lambda i, f, te, na: (0, 0)),
                      pl.BlockSpec((None, D, tf), lambda i, f, te, na: (te[i], 0, fcol(i, f, na))),
                      pl.BlockSpec((None, D, tf), lambda i, f, te, na: (te[i], 0, fcol(i, f, na))),
                      pl.BlockSpec((None, tf, D), lambda i, f, te, na: (te[i], fcol(i, f, na), 0))],
            out_specs=pl.BlockSpec((tm, D), lambda i, f, te, na: (i, 0)),
            scratch_shapes=[pltpu.VMEM((tm, D), BF16)]),
        out_shape=jax.ShapeDtypeStruct((P, D), F32),
        compiler_params=_params("arbitrary", "arbitrary"),
        name="moe_experts",
    )(tile_expert, n_active, xs, g.reshape(1, D), wg, wu, wd)


def _combine_kernel(pos_ref, x_ref, mg_ref, gf_ref, y_ref, o_ref, ybuf, sem, *, tm):
    def issue(r, c):
        _row_copy(y_ref, pos_ref[0, 2 * r], ybuf.at[0], r, sem).start()
        _row_copy(y_ref, pos_ref[0, 2 * r + 1], ybuf.at[1], r, sem).start()
        return c

    def drain(r, c):
        _row_copy(y_ref, pos_ref[0, 2 * r], ybuf.at[0], r, sem).wait()
        _row_copy(y_ref, pos_ref[0, 2 * r + 1], ybuf.at[1], r, sem).wait()
        return c

    lax.fori_loop(0, tm, issue, 0, unroll=8)
    lax.fori_loop(0, tm, drain, 0, unroll=8)
    mg = mg_ref[...]
    v = x_ref[...] + mg[:, 0:1] * ybuf[0] + mg[:, 1:2] * ybuf[1]
    o_ref[...] = _rms(v, gf_ref[...])


def moe_combine_norm(x, y, pos, gates, final_g, tm=256):
    T, D = x.shape
    tm = _tile(T, tm)
    pos3 = pos.reshape(T // tm, 1, 2 * tm)
    return pl.pallas_call(
        functools.partial(_combine_kernel, tm=tm),
        grid=(T // tm,),
        in_specs=[pl.BlockSpec((None, 1, 2 * tm), lambda i: (i, 0, 0), memory_space=pltpu.SMEM),
                  pl.BlockSpec((tm, D), lambda i: (i, 0)),
                  pl.BlockSpec((tm, LANES), lambda i: (i, 0)),
                  pl.BlockSpec((1, D), lambda i: (0, 0)),
                  pl.BlockSpec(memory_space=pl.ANY)],
        out_specs=pl.BlockSpec((tm, D), lambda i: (i, 0)),
        out_shape=jax.ShapeDtypeStruct((T, D), F32),
        scratch_shapes=[pltpu.VMEM((2, tm, D), F32), pltpu.SemaphoreType.DMA(())],
        compiler_params=_params("arbitrary"),
        name="moe_combine_norm",
    )(pos3, x, gates, final_g.reshape(1, D), y)


def routed_moe_norm(x, g, w_router, b_router, wg, wu, wd, final_g, tm_e=512):
    T, D = x.shape
    tm_e = _tile(T, tm_e)
    mi, mg, cnt = router(x, g, w_router, b_router)
    cnt = cnt[0, :N_EXPERTS].astype(jnp.int32)
    padded = (cnt + tm_e - 1) // tm_e * tm_e
    ends = jnp.cumsum(padded)
    off = ends - padded
    pos = jnp.stack([off[mi[:, 0]] + mi[:, 2], off[mi[:, 1]] + mi[:, 3]], axis=1)
    n_rows = 2 * T + N_EXPERTS * tm_e
    n_active = (ends[-1:] // tm_e).astype(jnp.int32)
    starts = jnp.minimum(jnp.arange(n_rows // tm_e, dtype=jnp.int32), n_active - 1) * tm_e
    tile_expert = jnp.sum(starts[:, None] >= ends[None, :], axis=1).astype(jnp.int32)
    grp = jnp.concatenate([cnt, padded, off, ends[-1:]]).astype(jnp.int32)
    xs = moe_dispatch(x, pos, grp, n_rows)
    y = moe_experts(xs, g, tile_expert, n_active, wg, wu, wd, tm_e)
    return moe_combine_norm(x, y, pos, mg, final_g)


def _pad_heads(w, heads, dim, to=LANES):
    lead = w.shape[:-1]
    w = w.reshape(lead + (heads, dim))
    w = jnp.pad(w, [(0, 0)] * len(lead) + [(0, 0), (0, to - dim)])
    return w.reshape(lead + (heads * to,))


def _swap_halves(w):
    half = w.shape[-1] // 2
    return jnp.concatenate([w[..., half:], w[..., :half]], axis=-1)


def _even_w_in(w):
    c_q = w[:, :MLA_Q_RANK]
    o = MLA_Q_RANK
    c_kv = w[:, o:o + MLA_KV_RANK]
    o += MLA_KV_RANK
    k_rope = w[:, o:o + MLA_ROPE]
    o += MLA_ROPE
    q_s = w[:, o:o + SWA_HEADS * SWA_HEAD_DIM]
    o += SWA_HEADS * SWA_HEAD_DIM
    k_s = w[:, o:o + SWA_KV_HEADS * SWA_HEAD_DIM]
    o += SWA_KV_HEADS * SWA_HEAD_DIM
    v_s = w[:, o:o + SWA_KV_HEADS * SWA_HEAD_DIM]
    pad = lambda a: jnp.pad(a, ((0, 0), (0, LANES - a.shape[1])))
    return jnp.concatenate([
        _pad_heads(q_s, SWA_HEADS, SWA_HEAD_DIM),
        _pad_heads(k_s, SWA_KV_HEADS, SWA_HEAD_DIM),
        _pad_heads(v_s, SWA_KV_HEADS, SWA_HEAD_DIM),
        c_q, c_kv, pad(k_rope), pad(_swap_halves(k_rope))], axis=1).astype(BF16)


def _t5_bucket(dist):
    max_exact = REL_BUCKETS // 2
    large = max_exact + (jnp.log(jnp.maximum(dist, 1).astype(F32) / max_exact)
                         / math.log(REL_MAX_DIST / max_exact) * (REL_BUCKETS - max_exact)).astype(jnp.int32)
    large = jnp.minimum(large, REL_BUCKETS - 1)
    return jnp.where(dist < max_exact, dist, large)


def _swa_bias_table(rel_bias):
    i = jnp.arange(WINDOW)[:, None]
    j = jnp.arange(2 * WINDOW)[None, :]
    rel = WINDOW + i - j
    bias = rel_bias[_t5_bucket(jnp.maximum(rel, 0))].astype(F32)
    valid = (rel >= 0) & (rel < WINDOW)
    return jnp.where(valid[None], jnp.transpose(bias, (2, 0, 1)), NEG)


def _rope_tables(positions):
    half = MLA_ROPE // 2
    inv = ROPE_THETA ** (-jnp.arange(half, dtype=F32) / half)
    ang = positions.astype(F32).reshape(-1, 1) * inv
    cos, sin = jnp.cos(ang), jnp.sin(ang)
    z = jnp.zeros((ang.shape[0], LANES - MLA_ROPE), F32)
    return jnp.concatenate([cos, cos, z], axis=1), jnp.concatenate([-sin, sin, z], axis=1)


def kernel(x, mem, positions, rel_bias, norm_mix, norm_cross, norm_mem, norm_ffn, x_wq, x_wk, x_wv, x_wo,
           ev_w_in, ev_mla_q_norm, ev_mla_w_uq, ev_mla_kv_norm, ev_mla_w_ukv, ev_swa_sinks, ev_w_o,
           ev_ffn_w_gate, ev_ffn_w_up, ev_ffn_w_down, od_w_in, od_fox_b_f, od_w_o, od_router_w, od_router_b,
           od_moe_w_gate, od_moe_w_up, od_moe_w_down, final_norm_g):
    B, S, D = x.shape
    T = B * S
    M = mem.shape[1]
    xs = x.reshape(T, D)
    mem2 = mem.reshape(B * M, D)
    bf = lambda a: a.astype(BF16)

    def cross(xs, layer):
        kvw = bf(jnp.concatenate([x_wk[layer], x_wv[layer]], axis=1))
        kv = norm_matmul(mem2, norm_mem[layer], kvw, tm=256, tn=1024).reshape(B, M, -1)
        return cross_attention(xs, norm_cross[layer], bf(x_wq[layer]), kv, bf(x_wo[layer]), B, S)

    z0 = norm_matmul(xs, norm_mix[0], _even_w_in(ev_w_in[0]), tm=512, tn=2048)
    cos2, sin2 = _rope_tables(positions)
    w_uq = ev_mla_w_uq[0].reshape(MLA_Q_RANK, MLA_HEADS, MLA_NOPE + MLA_ROPE)
    rope_w = w_uq[..., MLA_NOPE:]
    padr = lambda a: jnp.pad(a, ((0, 0), (0, 0), (0, LANES - MLA_ROPE)))
    wq_all = bf(jnp.concatenate([w_uq[..., :MLA_NOPE], padr(rope_w)], axis=-1).reshape(MLA_Q_RANK, -1))
    wq_swap = bf(padr(_swap_halves(rope_w)).reshape(MLA_Q_RANK, -1))
    w_ukv = ev_mla_w_ukv[0].reshape(MLA_KV_RANK, MLA_HEADS, MLA_NOPE + MLA_V)
    wk = bf(w_ukv[..., :MLA_NOPE].reshape(MLA_KV_RANK, -1))
    wv = bf(w_ukv[..., MLA_NOPE:].reshape(MLA_KV_RANK, -1))
    q, k, v = mla_prep(z0, cos2, sin2, ev_mla_q_norm[0], ev_mla_kv_norm[0], wq_all, wq_swap, wk, wv)
    tq, tk = _attn_tiles(S)
    o_mla = flash_attention(q.reshape(B, S, -1), k.reshape(B, S, -1), v.reshape(B, S, -1), None,
                            heads=MLA_HEADS, dk=256, dv=MLA_V, q_off=0, k_off=0, v_off=0, tq=tq, tk=tk)
    o_swa = swa_attention(z0, _swa_bias_table(rel_bias), ev_swa_sinks[0], B, S)
    n_mla = MLA_HEADS * MLA_V
    w_o_swa = ev_w_o[0][n_mla:].reshape(SWA_HEADS, SWA_HEAD_DIM, D)
    w_o_swa = jnp.pad(w_o_swa, ((0, 0), (0, LANES - SWA_HEAD_DIM), (0, 0))).reshape(SWA_HEADS * LANES, D)
    xs = matmul_res(xs, [(o_mla.reshape(T, n_mla), bf(ev_w_o[0][:n_mla])), (o_swa, bf(w_o_swa))])
    xs = cross(xs, 0)
    xs = ffn(xs, norm_ffn[0], bf(ev_ffn_w_gate[0]), bf(ev_ffn_w_up[0]), bf(ev_ffn_w_down[0]))

    FW = FOX_HEADS * FOX_HEAD_DIM
    col_scale = jnp.concatenate([jnp.full((1, FW), LOG2E * FOX_HEAD_DIM ** -0.5, F32), jnp.ones((1, 2 * FW), F32)], axis=1)
    w_f = jnp.pad(od_w_in[0][:, 3 * FW:], ((0, 0), (0, LANES - FOX_HEADS)))
    z1, f_logit = norm_matmul(xs, norm_mix[1], bf(od_w_in[0][:, :3 * FW]), col_scale=col_scale, side_w=w_f,
                              tm=512, tn=2048)
    b_f = jnp.pad(od_fox_b_f[0], (0, LANES - FOX_HEADS)).reshape(1, LANES)
    decay = decay_cumsum(f_logit, b_f, B, S)[:, :FOX_HEADS, :].reshape(B, FOX_HEADS, S // tk, tk)
    z1 = z1.reshape(B, S, 3 * FW)
    o_fox = flash_attention(z1, z1, z1, decay, heads=FOX_HEADS, dk=FOX_HEAD_DIM, dv=FOX_HEAD_DIM,
                            q_off=0, k_off=FOX_HEADS, v_off=2 * FOX_HEADS, tq=tq, tk=tk)
    xs = matmul_res(xs, [(o_fox.reshape(T, FW), bf(od_w_o[0]))])
    xs = cross(xs, 1)
    out = routed_moe_norm(xs, norm_ffn[1], od_router_w[0], od_router_b[0], bf(od_moe_w_gate[0]),
                          bf(od_moe_w_up[0]), bf(od_moe_w_down[0]), final_norm_g)
    return out.reshape(B, S, D)
```

```python
import functools
import math

import jax
import jax.numpy as jnp
from jax import lax
from jax.experimental import pallas as pl
from jax.experimental.pallas import tpu as pltpu

F32 = jnp.float32
BF16 = jnp.bfloat16

EPS = 1e-6
NEG = -1e30
LANES = 128
LOG2E = math.log2(math.e)

MLA_HEADS = 8
MLA_Q_RANK = 768
MLA_KV_RANK = 512
MLA_NOPE = 128
MLA_ROPE = 64
MLA_V = 128
ROPE_THETA = 10000.0
SWA_HEADS = 16
SWA_KV_HEADS = 2
SWA_GROUP = SWA_HEADS // SWA_KV_HEADS
SWA_HEAD_DIM = 64
WINDOW = 128
REL_BUCKETS = 32
REL_MAX_DIST = 128
FOX_HEADS = 16
FOX_HEAD_DIM = 128
X_HEADS = 4
X_HEAD_DIM = 128
N_EXPERTS = 8

VMEM_LIMIT_BYTES = 56 * 1024 * 1024


def _tile(n, pref):
    return pref if n % pref == 0 else n


def _params(*sem, flags=None):
    return pltpu.CompilerParams(dimension_semantics=sem, vmem_limit_bytes=VMEM_LIMIT_BYTES, flags=flags)


def _rms(xf, g):
    ms = jnp.mean(xf * xf, axis=-1, keepdims=True)
    return xf * lax.rsqrt(ms + EPS) * g


def _split3(v):
    hi = v.astype(BF16)
    r = v - hi.astype(F32)
    mid = r.astype(BF16)
    lo = (r - mid.astype(F32)).astype(BF16)
    return hi, mid, lo


def _dot(a, b):
    return jnp.dot(a, b, preferred_element_type=F32)


def _dot_nt(a, b):
    return lax.dot_general(a, b, (((1,), (1,)), ((), ())), preferred_element_type=F32)


def _norm_matmul_kernel(x_ref, g_ref, w_ref, cs_ref, *rest, side):
    if side:
        wsh_ref, wsl_ref, o_ref, so_ref, h_sc = rest
    else:
        o_ref, h_sc = rest

    @pl.when(pl.program_id(1) == 0)
    def _():
        hn = _rms(x_ref[...], g_ref[...])
        h_hi = hn.astype(BF16)
        h_sc[...] = h_hi
        if side:
            h_lo = (hn - h_hi.astype(F32)).astype(BF16)
            so_ref[...] = (_dot(h_hi, wsh_ref[...]) + _dot(h_lo, wsh_ref[...])
                           + _dot(h_hi, wsl_ref[...]))

    o_ref[...] = (_dot(h_sc[...], w_ref[...]) * cs_ref[...]).astype(o_ref.dtype)


def norm_matmul(x, g, w, col_scale=None, side_w=None, tm=512, tn=2048):
    T, D = x.shape
    N = w.shape[1]
    tm, tn = _tile(T, tm), _tile(N, tn)
    if col_scale is None:
        col_scale = jnp.ones((1, N), F32)
    in_specs = [pl.BlockSpec((tm, D), lambda i, j: (i, 0)),
                pl.BlockSpec((1, D), lambda i, j: (0, 0)),
                pl.BlockSpec((D, tn), lambda i, j: (0, j)),
                pl.BlockSpec((1, tn), lambda i, j: (0, j))]
    args = [x, g.reshape(1, D), w, col_scale]
    out_shape = [jax.ShapeDtypeStruct((T, N), BF16)]
    out_specs = [pl.BlockSpec((tm, tn), lambda i, j: (i, j))]
    side = side_w is not None
    if side:
        ws_hi = side_w.astype(BF16)
        ws_lo = (side_w - ws_hi.astype(F32)).astype(BF16)
        ns = side_w.shape[1]
        in_specs += [pl.BlockSpec((D, ns), lambda i, j: (0, 0))] * 2
        args += [ws_hi, ws_lo]
        out_shape.append(jax.ShapeDtypeStruct((T, ns), F32))
        out_specs.append(pl.BlockSpec((tm, ns), lambda i, j: (i, 0)))
    res = pl.pallas_call(
        functools.partial(_norm_matmul_kernel, side=side),
        grid=(T // tm, N // tn),
        in_specs=in_specs, out_specs=out_specs, out_shape=out_shape,
        scratch_shapes=[pltpu.VMEM((tm, D), BF16)],
        compiler_params=_params("parallel", "arbitrary"),
        name="norm_matmul",
    )(*args)
    return res if side else res[0]


def _matmul_res_kernel(*refs, n):
    x_ref = refs[0]
    o_ref = refs[1 + 2 * n]
    acc = x_ref[...]
    for k in range(n):
        acc = acc + _dot(refs[1 + 2 * k][...], refs[2 + 2 * k][...])
    o_ref[...] = acc


def matmul_res(x, pairs, tm=512, tn=1024):
    T, N = x.shape
    tm, tn = _tile(T, tm), _tile(N, tn)
    in_specs = [pl.BlockSpec((tm, tn), lambda i, j: (i, j))]
    args = [x]
    for a, w in pairs:
        K = a.shape[1]
        in_specs += [pl.BlockSpec((tm, K), lambda i, j: (i, 0)),
                     pl.BlockSpec((K, tn), lambda i, j: (0, j))]
        args += [a, w]
    return pl.pallas_call(
        functools.partial(_matmul_res_kernel, n=len(pairs)),
        grid=(T // tm, N // tn),
        in_specs=in_specs,
        out_specs=pl.BlockSpec((tm, tn), lambda i, j: (i, j)),
        out_shape=jax.ShapeDtypeStruct((T, N), F32),
        compiler_params=_params("parallel", "parallel"),
        name="matmul_res",
    )(*args)


def _mla_prep_kernel(z_ref, cos_ref, sin_ref, qg_ref, kvg_ref, wq_ref, wqs_ref, wk_ref, wv_ref,
                     q_ref, k_ref, v_ref, *, scale):
    cos2 = cos_ref[...]
    sin2 = sin_ref[...]
    cq = _rms(z_ref[:, 512:1280].astype(F32), qg_ref[...]).astype(BF16)
    qa = _dot(cq, wq_ref[...])
    qs = _dot(cq, wqs_ref[...])
    for h in range(MLA_HEADS):
        a, b = h * 256, h * 256 + 128
        q_ref[:, a:b] = (qa[:, a:b] * scale).astype(BF16)
        q_ref[:, b:b + 128] = ((qa[:, b:b + 128] * cos2 + qs[:, h * 128:(h + 1) * 128] * sin2)
                               * scale).astype(BF16)
    ckv = _rms(z_ref[:, 1280:1792].astype(F32), kvg_ref[...]).astype(BF16)
    kn = _dot(ckv, wk_ref[...])
    v_ref[...] = _dot(ckv, wv_ref[...]).astype(BF16)
    kpe = (z_ref[:, 1792:1920].astype(F32) * cos2 + z_ref[:, 1920:2048].astype(F32) * sin2).astype(BF16)
    for h in range(MLA_HEADS):
        k_ref[:, h * 256:h * 256 + 128] = kn[:, h * 128:(h + 1) * 128].astype(BF16)
        k_ref[:, h * 256 + 128:(h + 1) * 256] = kpe


def mla_prep(z, cos2, sin2, q_norm, kv_norm, wq_all, wq_swap, wk, wv, tm=512):
    T = z.shape[0]
    tm = _tile(T, tm)
    full = lambda a: pl.BlockSpec(a.shape, lambda i: (0, 0))
    qg, kvg = q_norm.reshape(1, -1), kv_norm.reshape(1, -1)
    return pl.pallas_call(
        functools.partial(_mla_prep_kernel, scale=LOG2E * (MLA_NOPE + MLA_ROPE) ** -0.5),
        grid=(T // tm,),
        in_specs=[pl.BlockSpec((tm, 2048), lambda i: (i, 1)),
                  pl.BlockSpec((tm, LANES), lambda i: (i, 0)),
                  pl.BlockSpec((tm, LANES), lambda i: (i, 0)),
                  full(qg), full(kvg), full(wq_all), full(wq_swap), full(wk), full(wv)],
        out_specs=[pl.BlockSpec((tm, MLA_HEADS * 256), lambda i: (i, 0)),
                   pl.BlockSpec((tm, MLA_HEADS * 256), lambda i: (i, 0)),
                   pl.BlockSpec((tm, MLA_HEADS * MLA_V), lambda i: (i, 0))],
        out_shape=[jax.ShapeDtypeStruct((T, MLA_HEADS * 256), BF16),
                   jax.ShapeDtypeStruct((T, MLA_HEADS * 256), BF16),
                   jax.ShapeDtypeStruct((T, MLA_HEADS * MLA_V), BF16)],
        compiler_params=_params("parallel"),
        name="mla_prep",
    )(z, cos2, sin2, qg, kvg, wq_all, wq_swap, wk, wv)


def _flash_kernel(*refs, tq, tk, groups, has_bias):
    if has_bias:
        q_ref, k_ref, v_ref, kb_ref, o_ref, m_sc, acc_sc, s_sc = refs
    else:
        q_ref, k_ref, v_ref, o_ref, m_sc, acc_sc, s_sc = refs
        kb_ref = None
    i = pl.program_id(2)
    rg = tq // groups
    dv = v_ref.shape[-1]
    m_sc[...] = jnp.full(m_sc.shape, NEG, F32)
    acc_sc[...] = jnp.zeros(acc_sc.shape, F32)
    ones = jnp.ones((tk, LANES), BF16)

    def scores(j, slot, diag=False):
        k = k_ref[pl.ds(pl.multiple_of(j * tk, tk), tk), :]
        kb = kb_ref[pl.ds(j, 1), :] if has_bias else None
        for g in range(groups):
            rows = slice(g * rg, (g + 1) * rg)
            kw = (g + 1) * rg if diag else tk
            s = _dot_nt(q_ref[rows, :], k[:kw])
            if has_bias:
                s = s - kb[:, :kw]
            s_sc[slot, rows, :kw] = s

    def consume(j, slot, diag):
        v1 = jnp.concatenate([v_ref[pl.ds(pl.multiple_of(j * tk, tk), tk), :], ones], axis=1)
        for g in range(groups):
            rows = slice(g * rg, (g + 1) * rg)
            kw = (g + 1) * rg if diag else tk
            s = s_sc[slot, rows, :kw]
            if diag:
                qpos = g * rg + lax.broadcasted_iota(jnp.int32, (rg, kw), 0)
                kpos = lax.broadcasted_iota(jnp.int32, (rg, kw), 1)
                s = jnp.where(kpos <= qpos, s, NEG)
            m_prev = m_sc[rows, :]
            m_new = jnp.maximum(m_prev, jnp.max(s, axis=-1, keepdims=True))
            alpha = jnp.exp2(m_prev - m_new)
            p = jnp.exp2((s - m_new).astype(BF16))
            acc_sc[rows, :] = alpha * acc_sc[rows, :] + _dot(p, v1[:kw])
            m_sc[rows, :] = m_new

    n_full = i
    scores(0, 0)

    def pair(jj, c):
        j = 2 * jj
        scores(j + 1, 1)
        consume(j, 0, False)
        scores(j + 2, 0)
        consume(j + 1, 1, False)
        return c

    lax.fori_loop(0, n_full // 2, pair, 0)
    r = 2 * (n_full // 2)

    @pl.when(n_full % 2 == 1)
    def _():
        scores(r + 1, 1, diag=True)
        consume(r, 0, False)
        consume(r + 1, 1, True)

    @pl.when(n_full % 2 == 0)
    def _():
        consume(r, 0, True)

    o_ref[...] = (acc_sc[:, :dv] / acc_sc[:, dv:]).astype(o_ref.dtype)


def flash_attention(q_arr, k_arr, v_arr, kbias, *, heads, dk, dv, q_off, k_off, v_off, tq, tk, groups=8):
    B, S = q_arr.shape[:2]
    assert S % tk == 0 and tk == tq and tq % groups == 0
    has_bias = kbias is not None
    in_specs = [pl.BlockSpec((None, tq, dk), lambda b, h, i: (b, i, q_off + h)),
                pl.BlockSpec((None, S, dk), lambda b, h, i: (b, 0, k_off + h)),
                pl.BlockSpec((None, S, dv), lambda b, h, i: (b, 0, v_off + h))]
    args = [q_arr, k_arr, v_arr]
    if has_bias:
        in_specs.append(pl.BlockSpec((None, None, S // tk, tk), lambda b, h, i: (b, h, 0, 0)))
        args.append(kbias)
    return pl.pallas_call(
        functools.partial(_flash_kernel, tq=tq, tk=tk, groups=groups, has_bias=has_bias),
        grid=(B, heads, S // tq),
        in_specs=in_specs,
        out_specs=pl.BlockSpec((None, tq, dv), lambda b, h, i: (b, i, h)),
        out_shape=jax.ShapeDtypeStruct((B, S, heads * dv), BF16),
        scratch_shapes=[pltpu.VMEM((tq, 1), F32), pltpu.VMEM((tq, dv + LANES), F32),
                        pltpu.VMEM((2, tq, tk), F32)],
        compiler_params=_params("parallel", "parallel", "arbitrary"),
        name="flash_attention",
    )(*args)


def _attn_tiles(S):
    tk = _tile(S, 1024)
    return tk, tk


def _swa_kernel(sink_ref, q_ref, kvc_ref, kvp_ref, bias_ref, o_ref):
    i = pl.program_id(1)
    lane = lax.broadcasted_iota(jnp.int32, (1, 2 * WINDOW), 1)
    pen = jnp.where((lane < WINDOW) & (i == 0), NEG, 0.0).astype(F32)
    for g in range(SWA_KV_HEADS):
        kcat = jnp.concatenate([kvp_ref[:, g * 128:(g + 1) * 128], kvc_ref[:, g * 128:(g + 1) * 128]], axis=0)
        vcat = jnp.concatenate([kvp_ref[:, 256 + g * 128:256 + (g + 1) * 128],
                                kvc_ref[:, 256 + g * 128:256 + (g + 1) * 128]], axis=0)
        for hh in range(SWA_GROUP):
            h = g * SWA_GROUP + hh
            s = _dot_nt(q_ref[:, h * 128:(h + 1) * 128], kcat) * (SWA_HEAD_DIM ** -0.5) + bias_ref[h] + pen
            sink = sink_ref[h]
            m = jnp.maximum(jnp.max(s, axis=-1, keepdims=True), sink)
            p = jnp.exp(s - m)
            denom = jnp.sum(p, axis=-1, keepdims=True) + jnp.exp(sink - m)
            o = _dot(p.astype(BF16), vcat) / denom
            o_ref[:, h * 128:(h + 1) * 128] = o.astype(o_ref.dtype)


def swa_attention(z, bias_tab, sinks, B, S):
    T = z.shape[0]
    nb = S // WINDOW
    return pl.pallas_call(
        _swa_kernel,
        grid=(B, nb),
        in_specs=[pl.BlockSpec(memory_space=pltpu.SMEM),
                  pl.BlockSpec((WINDOW, 2048), lambda b, i: (b * nb + i, 0)),
                  pl.BlockSpec((WINDOW, 512), lambda b, i: (b * nb + i, 4)),
                  pl.BlockSpec((WINDOW, 512), lambda b, i: (b * nb + jnp.maximum(i - 1, 0), 4)),
                  pl.BlockSpec(bias_tab.shape, lambda b, i: (0, 0, 0))],
        out_specs=pl.BlockSpec((WINDOW, 2048), lambda b, i: (b * nb + i, 0)),
        out_shape=jax.ShapeDtypeStruct((T, 2048), BF16),
        compiler_params=_params("parallel", "arbitrary"),
        name="swa_attention",
    )(sinks, z, z, z, bias_tab)


def _decay_kernel(f_ref, b_ref, o_ref, carry_sc, *, tm):
    @pl.when(pl.program_id(1) == 0)
    def _():
        carry_sc[...] = jnp.zeros(carry_sc.shape, F32)

    x = f_ref[...] + b_ref[...]
    lf = jnp.minimum(x, 0.0) - jnp.log1p(jnp.exp(-jnp.abs(x)))
    row = lax.broadcasted_iota(jnp.int32, (tm, tm), 0)
    col = lax.broadcasted_iota(jnp.int32, (tm, tm), 1)
    tri = jnp.where(col <= row, 1.0, 0.0).astype(BF16)
    hi, mid, lo = _split3(lf)
    cs = _dot(tri, hi) + _dot(tri, mid) + _dot(tri, lo) + carry_sc[...]
    carry_sc[...] = cs[tm - 1:tm, :]
    o_ref[...] = (cs * LOG2E).T


def decay_cumsum(f_logit, b_f, B, S, tm=512):
    tm = _tile(S, tm)
    ns = S // tm
    return pl.pallas_call(
        functools.partial(_decay_kernel, tm=tm),
        grid=(B, ns),
        in_specs=[pl.BlockSpec((tm, LANES), lambda b, i: (b * ns + i, 0)),
                  pl.BlockSpec((1, LANES), lambda b, i: (0, 0))],
        out_specs=pl.BlockSpec((None, LANES, tm), lambda b, i: (b, 0, i)),
        out_shape=jax.ShapeDtypeStruct((B, LANES, S), F32),
        scratch_shapes=[pltpu.VMEM((1, LANES), F32)],
        compiler_params=_params("parallel", "arbitrary"),
        name="decay_cumsum",
    )(f_logit, b_f)


def _cross_kernel(x_ref, g_ref, wq_ref, k_ref, v_ref, wo_ref, o_ref):
    x = x_ref[...]
    hn = _rms(x, g_ref[...]).astype(BF16)
    q = (_dot(hn, wq_ref[...]) * (X_HEAD_DIM ** -0.5)).astype(BF16)
    outs = []
    for h in range(X_HEADS):
        sl = slice(h * X_HEAD_DIM, (h + 1) * X_HEAD_DIM)
        s = _dot_nt(q[:, sl], k_ref[:, sl])
        m = jnp.max(s, axis=-1, keepdims=True)
        p = jnp.exp(s - m)
        o = _dot(p.astype(BF16), v_ref[:, sl]) / jnp.sum(p, axis=-1, keepdims=True)
        outs.append(o.astype(BF16))
    o_ref[...] = x + _dot(jnp.concatenate(outs, axis=1), wo_ref[...])


def cross_attention(x, g, wq, kv, wo, B, S, tm=512):
    T, D = x.shape
    M = kv.shape[1]
    XW = X_HEADS * X_HEAD_DIM
    tm = _tile(S, tm)
    ns = S // tm
    return pl.pallas_call(
        _cross_kernel,
        grid=(B, ns),
        in_specs=[pl.BlockSpec((tm, D), lambda b, i: (b * ns + i, 0)),
                  pl.BlockSpec((1, D), lambda b, i: (0, 0)),
                  pl.BlockSpec((D, XW), lambda b, i: (0, 0)),
                  pl.BlockSpec((None, M, XW), lambda b, i: (b, 0, 0)),
                  pl.BlockSpec((None, M, XW), lambda b, i: (b, 0, 1)),
                  pl.BlockSpec((XW, D), lambda b, i: (0, 0))],
        out_specs=pl.BlockSpec((tm, D), lambda b, i: (b * ns + i, 0)),
        out_shape=jax.ShapeDtypeStruct((T, D), F32),
        compiler_params=_params("parallel", "parallel"),
        name="cross_attention",
    )(x, g.reshape(1, D), wq, kv, kv, wo)


def _ffn_kernel(x_ref, g_ref, wg_ref, wu_ref, wd_ref, o_ref, h_sc):
    @pl.when(pl.program_id(1) == 0)
    def _():
        x = x_ref[...]
        h_sc[...] = _rms(x, g_ref[...]).astype(BF16)
        o_ref[...] = x

    h = h_sc[...]
    gate = _dot(h, wg_ref[...])
    up = _dot(h, wu_ref[...])
    a = (gate * jax.nn.sigmoid(gate) * up).astype(BF16)
    o_ref[...] += _dot(a, wd_ref[...])


def ffn(x, g, wg, wu, wd, tm=512, tf=1024):
    T, D = x.shape
    F = wg.shape[1]
    tm, tf = _tile(T, tm), _tile(F, tf)
    return pl.pallas_call(
        _ffn_kernel,
        grid=(T // tm, F // tf),
        in_specs=[pl.BlockSpec((tm, D), lambda i, f: (i, 0)),
                  pl.BlockSpec((1, D), lambda i, f: (0, 0)),
                  pl.BlockSpec((D, tf), lambda i, f: (0, f)),
                  pl.BlockSpec((D, tf), lambda i, f: (0, f)),
                  pl.BlockSpec((tf, D), lambda i, f: (f, 0))],
        out_specs=pl.BlockSpec((tm, D), lambda i, f: (i, 0)),
        out_shape=jax.ShapeDtypeStruct((T, D), F32),
        scratch_shapes=[pltpu.VMEM((tm, D), BF16)],
        compiler_params=_params("parallel", "arbitrary"),
        name="ffn",
    )(x, g.reshape(1, D), wg, wu, wd)


def _router_kernel(x_ref, g_ref, wh_ref, wl_ref, b_ref, mi_ref, mg_ref, cnt_ref, carry_sc, *, tm):
    @pl.when(pl.program_id(0) == 0)
    def _():
        carry_sc[...] = jnp.zeros(carry_sc.shape, F32)

    hn = _rms(x_ref[...], g_ref[...])
    h_hi = hn.astype(BF16)
    h_lo = (hn - h_hi.astype(F32)).astype(BF16)
    logits = (_dot(h_hi, wh_ref[...]) + _dot(h_lo, wh_ref[...]) + _dot(h_hi, wl_ref[...])) + b_ref[...]
    lane = lax.broadcasted_iota(jnp.int32, logits.shape, 1)
    logits = jnp.where(lane < N_EXPERTS, logits, -jnp.inf)
    m1 = jnp.max(logits, axis=-1, keepdims=True)
    i1 = jnp.min(jnp.where(logits == m1, lane, LANES), axis=-1, keepdims=True)
    oh1 = lane == i1
    l2 = jnp.where(oh1, -jnp.inf, logits)
    m2 = jnp.max(l2, axis=-1, keepdims=True)
    i2 = jnp.min(jnp.where(l2 == m2, lane, LANES), axis=-1, keepdims=True)
    oh2 = lane == i2
    e2 = jnp.exp(m2 - m1)
    g1 = 1.0 / (1.0 + e2)
    mg_ref[...] = jnp.where(lane == 0, g1, jnp.where(lane == 1, e2 * g1, 0.0))
    row = lax.broadcasted_iota(jnp.int32, (tm, tm), 0)
    col = lax.broadcasted_iota(jnp.int32, (tm, tm), 1)
    tri = jnp.where(col < row, 1.0, 0.0).astype(BF16)
    oh1f = jnp.where(oh1, 1.0, 0.0)
    oh2f = jnp.where(oh2, 1.0, 0.0)
    ex1 = _dot(tri, oh1f.astype(BF16))
    ex2 = _dot(tri, oh2f.astype(BF16))
    tot1 = jnp.sum(oh1f, axis=0, keepdims=True)
    tot2 = jnp.sum(oh2f, axis=0, keepdims=True)
    carry = carry_sc[...]
    r1 = jnp.sum(jnp.where(oh1, ex1 + carry, 0.0), axis=-1, keepdims=True).astype(jnp.int32)
    r2 = jnp.sum(jnp.where(oh2, ex2 + (carry + tot1), 0.0), axis=-1, keepdims=True).astype(jnp.int32)
    mi_ref[...] = jnp.where(lane == 0, i1, jnp.where(lane == 1, i2, jnp.where(lane == 2, r1,
                            jnp.where(lane == 3, r2, 0))))
    total = carry + tot1 + tot2
    carry_sc[...] = total
    cnt_ref[...] = total


def router(x, g, w_router, b_router, tm=512):
    T, D = x.shape
    tm = _tile(T, tm)
    wp = jnp.pad(w_router, ((0, 0), (0, LANES - N_EXPERTS)))
    w_hi = wp.astype(BF16)
    w_lo = (wp - w_hi.astype(F32)).astype(BF16)
    bp = jnp.pad(b_router, (0, LANES - N_EXPERTS)).reshape(1, LANES)
    return pl.pallas_call(
        functools.partial(_router_kernel, tm=tm),
        grid=(T // tm,),
        in_specs=[pl.BlockSpec((tm, D), lambda i: (i, 0)),
                  pl.BlockSpec((1, D), lambda i: (0, 0)),
                  pl.BlockSpec((D, LANES), lambda i: (0, 0)),
                  pl.BlockSpec((D, LANES), lambda i: (0, 0)),
                  pl.BlockSpec((1, LANES), lambda i: (0, 0))],
        out_specs=[pl.BlockSpec((tm, LANES), lambda i: (i, 0)),
                   pl.BlockSpec((tm, LANES), lambda i: (i, 0)),
                   pl.BlockSpec((1, LANES), lambda i: (0, 0))],
        out_shape=[jax.ShapeDtypeStruct((T, LANES), jnp.int32),
                   jax.ShapeDtypeStruct((T, LANES), F32),
                   jax.ShapeDtypeStruct((1, LANES), F32)],
        scratch_shapes=[pltpu.VMEM((1, LANES), F32)],
        compiler_params=_params("arbitrary"),
        name="router",
    )(x, g.reshape(1, D), w_hi, w_lo, bp)


def _row_copy(src_ref, src_row, dst_ref, dst_row, sem):
    return pltpu.make_async_copy(src_ref.at[pl.ds(src_row, 1), :], dst_ref.at[pl.ds(dst_row, 1), :], sem)


def _dispatch_kernel(grp_ref, pos_ref, x_ref, xs_ref, zero_sc, sem, *, tm):
    def issue(r, c):
        _row_copy(x_ref, r, xs_ref, pos_ref[0, 2 * r], sem).start()
        _row_copy(x_ref, r, xs_ref, pos_ref[0, 2 * r + 1], sem).start()
        return c

    lax.fori_loop(0, tm, issue, 0, unroll=8)

    def pad_rows(fn):
        for e in range(N_EXPERTS):
            cnt, padded, off = grp_ref[e], grp_ref[N_EXPERTS + e], grp_ref[2 * N_EXPERTS + e]
            lax.fori_loop(cnt, padded, lambda r, c: fn(off + r, c), 0)
        lax.fori_loop(grp_ref[3 * N_EXPERTS], xs_ref.shape[0], fn, 0)

    @pl.when(pl.program_id(0) == 0)
    def _():
        zero_sc[...] = jnp.zeros(zero_sc.shape, F32)

        def start(p, c):
            _row_copy(zero_sc, 0, xs_ref, p, sem).start()
            return c

        def wait(p, c):
            _row_copy(zero_sc, 0, xs_ref, p, sem).wait()
            return c

        pad_rows(start)
        pad_rows(wait)

    def drain(r, c):
        _row_copy(x_ref, r, xs_ref, pos_ref[0, 2 * r], sem).wait()
        _row_copy(x_ref, r, xs_ref, pos_ref[0, 2 * r + 1], sem).wait()
        return c

    lax.fori_loop(0, tm, drain, 0, unroll=8)


def moe_dispatch(x, pos, grp, n_rows, tm=256):
    T, D = x.shape
    tm = _tile(T, tm)
    pos3 = pos.reshape(T // tm, 1, 2 * tm)
    return pl.pallas_call(
        functools.partial(_dispatch_kernel, tm=tm),
        grid_spec=pltpu.PrefetchScalarGridSpec(
            num_scalar_prefetch=1,
            grid=(T // tm,),
            in_specs=[pl.BlockSpec((None, 1, 2 * tm), lambda i, grp: (i, 0, 0), memory_space=pltpu.SMEM),
                      pl.BlockSpec((tm, D), lambda i, grp: (i, 0))],
            out_specs=pl.BlockSpec(memory_space=pl.ANY),
            scratch_shapes=[pltpu.VMEM((8, D), F32), pltpu.SemaphoreType.DMA(())]),
        out_shape=jax.ShapeDtypeStruct((n_rows, D), F32),
        compiler_params=_params("arbitrary"),
        name="moe_dispatch",
    )(grp, pos3, x)


def _moe_expert_kernel(te_ref, nact_ref, xs_ref, g_ref, wg_ref, wu_ref, wd_ref, y_ref, h_sc):
    i, f = pl.program_id(0), pl.program_id(1)
    active = i < nact_ref[0]

    @pl.when(active & (f == 0))
    def _():
        h_sc[...] = _rms(xs_ref[...], g_ref[...]).astype(BF16)

    @pl.when(active)
    def _():
        h = h_sc[...]
        gate = _dot(h, wg_ref[...])
        up = _dot(h, wu_ref[...])
        a = (gate * jax.nn.sigmoid(gate) * up).astype(BF16)
        contrib = _dot(a, wd_ref[...])

        @pl.when(f == 0)
        def _():
            y_ref[...] = contrib

        @pl.when(f != 0)
        def _():
            y_ref[...] += contrib

    @pl.when(jnp.logical_not(active) & (f == 0))
    def _():
        y_ref[...] = jnp.zeros(y_ref.shape, F32)


def moe_experts(xs, g, tile_expert, n_active, wg, wu, wd, tm, tf=1024):
    P, D = xs.shape
    F = wg.shape[2]
    tf = _tile(F, tf)
    nf = F // tf
    row = lambda i, f, te, na: (jnp.minimum(i, na[0] - 1), 0)
    fcol = lambda i, f, na: jnp.where(i < na[0], f, nf - 1)
    return pl.pallas_call(
        _moe_expert_kernel,
        grid_spec=pltpu.PrefetchScalarGridSpec(
            num_scalar_prefetch=2,
            grid=(P // tm, nf),
            in_specs=[pl.BlockSpec((tm, D), row),
                      pl.BlockSpec((1, D), lambda i, f, te, na: (0, 0)),
                      pl.BlockSpec((None, D, tf), lambda i, f, te, na: (te[i], 0, fcol(i, f, na))),
                      pl.BlockSpec((None, D, tf), lambda i, f, te, na: (te[i], 0, fcol(i, f, na))),
                      pl.BlockSpec((None, tf, D), lambda i, f, te, na: (te[i], fcol(i, f, na), 0))],
            out_specs=pl.BlockSpec((tm, D), lambda i, f, te, na: (i, 0)),
            scratch_shapes=[pltpu.VMEM((tm, D), BF16)]),
        out_shape=jax.ShapeDtypeStruct((P, D), F32),
        compiler_params=_params("arbitrary", "arbitrary"),
        name="moe_experts",
    )(tile_expert, n_active, xs, g.reshape(1, D), wg, wu, wd)


def _combine_kernel(pos_ref, x_ref, mg_ref, gf_ref, y_ref, o_ref, ybuf, sem, *, tm):
    def issue(r, c):
        _row_copy(y_ref, pos_ref[0, 2 * r], ybuf.at[0], r, sem).start()
        _row_copy(y_ref, pos_ref[0, 2 * r + 1], ybuf.at[1], r, sem).start()
        return c

    def drain(r, c):
        _row_copy(y_ref, pos_ref[0, 2 * r], ybuf.at[0], r, sem).wait()
        _row_copy(y_ref, pos_ref[0, 2 * r + 1], ybuf.at[1], r, sem).wait()
        return c

    lax.fori_loop(0, tm, issue, 0, unroll=8)
    lax.fori_loop(0, tm, drain, 0, unroll=8)
    mg = mg_ref[...]
    v = x_ref[...] + mg[:, 0:1] * ybuf[0] + mg[:, 1:2] * ybuf[1]
    o_ref[...] = _rms(v, gf_ref[...])


def moe_combine_norm(x, y, pos, gates, final_g, tm=256):
    T, D = x.shape
    tm = _tile(T, tm)
    pos3 = pos.reshape(T // tm, 1, 2 * tm)
    return pl.pallas_call(
        functools.partial(_combine_kernel, tm=tm),
        grid=(T // tm,),
        in_specs=[pl.BlockSpec((None, 1, 2 * tm), lambda i: (i, 0, 0), memory_space=pltpu.SMEM),
                  pl.BlockSpec((tm, D), lambda i: (i, 0)),
                  pl.BlockSpec((tm, LANES), lambda i: (i, 0)),
                  pl.BlockSpec((1, D), lambda i: (0, 0)),
                  pl.BlockSpec(memory_space=pl.ANY)],
        out_specs=pl.BlockSpec((tm, D), lambda i: (i, 0)),
        out_shape=jax.ShapeDtypeStruct((T, D), F32),
        scratch_shapes=[pltpu.VMEM((2, tm, D), F32), pltpu.SemaphoreType.DMA(())],
        compiler_params=_params("arbitrary"),
        name="moe_combine_norm",
    )(pos3, x, gates, final_g.reshape(1, D), y)


def routed_moe_norm(x, g, w_router, b_router, wg, wu, wd, final_g, tm_e=512):
    T, D = x.shape
    tm_e = _tile(T, tm_e)
    mi, mg, cnt = router(x, g, w_router, b_router)
    cnt = cnt[0, :N_EXPERTS].astype(jnp.int32)
    padded = (cnt + tm_e - 1) // tm_e * tm_e
    ends = jnp.cumsum(padded)
    off = ends - padded
    first = lambda e: jnp.sum(jnp.where(e[:, None] == jnp.arange(N_EXPERTS)[None, :], off[None, :], 0), axis=1)
    pos = jnp.stack([first(mi[:, 0]) + mi[:, 2], first(mi[:, 1]) + mi[:, 3]], axis=1)
    n_rows = 2 * T + N_EXPERTS * tm_e
    n_active = (ends[-1:] // tm_e).astype(jnp.int32)
    starts = jnp.minimum(jnp.arange(n_rows // tm_e, dtype=jnp.int32), n_active - 1) * tm_e
    tile_expert = jnp.sum(starts[:, None] >= ends[None, :], axis=1).astype(jnp.int32)
    grp = jnp.concatenate([cnt, padded, off, ends[-1:]]).astype(jnp.int32)
    xs = moe_dispatch(x, pos, grp, n_rows)
    y = moe_experts(xs, g, tile_expert, n_active, wg, wu, wd, tm_e)
    return moe_combine_norm(x, y, pos, mg, final_g)


def _pad_heads(w, heads, dim, to=LANES):
    lead = w.shape[:-1]
    w = w.reshape(lead + (heads, dim))
    w = jnp.pad(w, [(0, 0)] * len(lead) + [(0, 0), (0, to - dim)])
    return w.reshape(lead + (heads * to,))


def _swap_halves(w):
    half = w.shape[-1] // 2
    return jnp.concatenate([w[..., half:], w[..., :half]], axis=-1)


def _even_w_in(w):
    c_q = w[:, :MLA_Q_RANK]
    o = MLA_Q_RANK
    c_kv = w[:, o:o + MLA_KV_RANK]
    o += MLA_KV_RANK
    k_rope = w[:, o:o + MLA_ROPE]
    o += MLA_ROPE
    q_s = w[:, o:o + SWA_HEADS * SWA_HEAD_DIM]
    o += SWA_HEADS * SWA_HEAD_DIM
    k_s = w[:, o:o + SWA_KV_HEADS * SWA_HEAD_DIM]
    o += SWA_KV_HEADS * SWA_HEAD_DIM
    v_s = w[:, o:o + SWA_KV_HEADS * SWA_HEAD_DIM]
    pad = lambda a: jnp.pad(a, ((0, 0), (0, LANES - a.shape[1])))
    return jnp.concatenate([
        _pad_heads(q_s, SWA_HEADS, SWA_HEAD_DIM),
        _pad_heads(k_s, SWA_KV_HEADS, SWA_HEAD_DIM),
        _pad_heads(v_s, SWA_KV_HEADS, SWA_HEAD_DIM),
        c_q, c_kv, pad(k_rope), pad(_swap_halves(k_rope))], axis=1).astype(BF16)


def _t5_bucket(dist):
    max_exact = REL_BUCKETS // 2
    large = max_exact + (jnp.log(jnp.maximum(dist, 1).astype(F32) / max_exact)
                         / math.log(REL_MAX_DIST / max_exact) * (REL_BUCKETS - max_exact)).astype(jnp.int32)
    large = jnp.minimum(large, REL_BUCKETS - 1)
    return jnp.where(dist < max_exact, dist, large)


def _swa_bias_table(rel_bias):
    i = jnp.arange(WINDOW)[:, None]
    j = jnp.arange(2 * WINDOW)[None, :]
    rel = WINDOW + i - j
    bias = rel_bias[_t5_bucket(jnp.maximum(rel, 0))].astype(F32)
    valid = (rel >= 0) & (rel < WINDOW)
    return jnp.where(valid[None], jnp.transpose(bias, (2, 0, 1)), NEG)


def _rope_tables(positions):
    half = MLA_ROPE // 2
    inv = ROPE_THETA ** (-jnp.arange(half, dtype=F32) / half)
    ang = positions.astype(F32).reshape(-1, 1) * inv
    cos, sin = jnp.cos(ang), jnp.sin(ang)
    z = jnp.zeros((ang.shape[0], LANES - MLA_ROPE), F32)
    return jnp.concatenate([cos, cos, z], axis=1), jnp.concatenate([-sin, sin, z], axis=1)


def kernel(x, mem, positions, rel_bias, norm_mix, norm_cross, norm_mem, norm_ffn, x_wq, x_wk, x_wv, x_wo,
           ev_w_in, ev_mla_q_norm, ev_mla_w_uq, ev_mla_kv_norm, ev_mla_w_ukv, ev_swa_sinks, ev_w_o,
           ev_ffn_w_gate, ev_ffn_w_up, ev_ffn_w_down, od_w_in, od_fox_b_f, od_w_o, od_router_w, od_router_b,
           od_moe_w_gate, od_moe_w_up, od_moe_w_down, final_norm_g):
    B, S, D = x.shape
    T = B * S
    M = mem.shape[1]
    xs = x.reshape(T, D)
    mem2 = mem.reshape(B * M, D)
    bf = lambda a: a.astype(BF16)

    def cross(xs, layer):
        kvw = bf(jnp.concatenate([x_wk[layer], x_wv[layer]], axis=1))
        kv = norm_matmul(mem2, norm_mem[layer], kvw, tm=256, tn=1024).reshape(B, M, -1)
        return cross_attention(xs, norm_cross[layer], bf(x_wq[layer]), kv, bf(x_wo[layer]), B, S)

    z0 = norm_matmul(xs, norm_mix[0], _even_w_in(ev_w_in[0]), tm=512, tn=2048)
    cos2, sin2 = _rope_tables(positions)
    w_uq = ev_mla_w_uq[0].reshape(MLA_Q_RANK, MLA_HEADS, MLA_NOPE + MLA_ROPE)
    rope_w = w_uq[..., MLA_NOPE:]
    padr = lambda a: jnp.pad(a, ((0, 0), (0, 0), (0, LANES - MLA_ROPE)))
    wq_all = bf(jnp.concatenate([w_uq[..., :MLA_NOPE], padr(rope_w)], axis=-1).reshape(MLA_Q_RANK, -1))
    wq_swap = bf(padr(_swap_halves(rope_w)).reshape(MLA_Q_RANK, -1))
    w_ukv = ev_mla_w_ukv[0].reshape(MLA_KV_RANK, MLA_HEADS, MLA_NOPE + MLA_V)
    wk = bf(w_ukv[..., :MLA_NOPE].reshape(MLA_KV_RANK, -1))
    wv = bf(w_ukv[..., MLA_NOPE:].reshape(MLA_KV_RANK, -1))
    q, k, v = mla_prep(z0, cos2, sin2, ev_mla_q_norm[0], ev_mla_kv_norm[0], wq_all, wq_swap, wk, wv)
    tq, tk = _attn_tiles(S)
    o_mla = flash_attention(q.reshape(B, S, -1), k.reshape(B, S, -1), v.reshape(B, S, -1), None,
                            heads=MLA_HEADS, dk=256, dv=MLA_V, q_off=0, k_off=0, v_off=0, tq=tq, tk=tk)
    o_swa = swa_attention(z0, _swa_bias_table(rel_bias), ev_swa_sinks[0], B, S)
    n_mla = MLA_HEADS * MLA_V
    w_o_swa = ev_w_o[0][n_mla:].reshape(SWA_HEADS, SWA_HEAD_DIM, D)
    w_o_swa = jnp.pad(w_o_swa, ((0, 0), (0, LANES - SWA_HEAD_DIM), (0, 0))).reshape(SWA_HEADS * LANES, D)
    xs = matmul_res(xs, [(o_mla.reshape(T, n_mla), bf(ev_w_o[0][:n_mla])), (o_swa, bf(w_o_swa))])
    xs = cross(xs, 0)
    xs = ffn(xs, norm_ffn[0], bf(ev_ffn_w_gate[0]), bf(ev_ffn_w_up[0]), bf(ev_ffn_w_down[0]))

    FW = FOX_HEADS * FOX_HEAD_DIM
    col_scale = jnp.concatenate([jnp.full((1, FW), LOG2E * FOX_HEAD_DIM ** -0.5, F32), jnp.ones((1, 2 * FW), F32)], axis=1)
    w_f = jnp.pad(od_w_in[0][:, 3 * FW:], ((0, 0), (0, LANES - FOX_HEADS)))
    z1, f_logit = norm_matmul(xs, norm_mix[1], bf(od_w_in[0][:, :3 * FW]), col_scale=col_scale, side_w=w_f,
                              tm=512, tn=2048)
    b_f = jnp.pad(od_fox_b_f[0], (0, LANES - FOX_HEADS)).reshape(1, LANES)
    decay = decay_cumsum(f_logit, b_f, B, S)[:, :FOX_HEADS, :].reshape(B, FOX_HEADS, S // tk, tk)
    z1 = z1.reshape(B, S, 3 * FW)
    o_fox = flash_attention(z1, z1, z1, decay, heads=FOX_HEADS, dk=FOX_HEAD_DIM, dv=FOX_HEAD_DIM,
                            q_off=0, k_off=FOX_HEADS, v_off=2 * FOX_HEADS, tq=tq, tk=tk)
    xs = matmul_res(xs, [(o_fox.reshape(T, FW), bf(od_w_o[0]))])
    xs = cross(xs, 1)
    out = routed_moe_norm(xs, norm_ffn[1], od_router_w[0], od_router_b[0], bf(od_moe_w_gate[0]),
                          bf(od_moe_w_up[0]), bf(od_moe_w_down[0]), final_norm_g)
    return out.reshape(B, S, D)
```

```python
import functools
import math

import jax
import jax.numpy as jnp
from jax import lax
from jax.experimental import pallas as pl
from jax.experimental.pallas import tpu as pltpu

F32 = jnp.float32
BF16 = jnp.bfloat16

EPS = 1e-6
NEG = -1e30
LANES = 128
LOG2E = math.log2(math.e)

MLA_HEADS = 8
MLA_Q_RANK = 768
MLA_KV_RANK = 512
MLA_NOPE = 128
MLA_ROPE = 64
MLA_V = 128
ROPE_THETA = 10000.0
SWA_HEADS = 16
SWA_KV_HEADS = 2
SWA_GROUP = SWA_HEADS // SWA_KV_HEADS
SWA_HEAD_DIM = 64
WINDOW = 128
REL_BUCKETS = 32
REL_MAX_DIST = 128
FOX_HEADS = 16
FOX_HEAD_DIM = 128
X_HEADS = 4
X_HEAD_DIM = 128
N_EXPERTS = 8

VMEM_LIMIT_BYTES = 56 * 1024 * 1024


def _tile(n, pref):
    return pref if n % pref == 0 else n


def _params(*sem, flags=None):
    return pltpu.CompilerParams(dimension_semantics=sem, vmem_limit_bytes=VMEM_LIMIT_BYTES, flags=flags)


def _rms(xf, g):
    ms = jnp.mean(xf * xf, axis=-1, keepdims=True)
    return xf * lax.rsqrt(ms + EPS) * g


def _split3(v):
    hi = v.astype(BF16)
    r = v - hi.astype(F32)
    mid = r.astype(BF16)
    lo = (r - mid.astype(F32)).astype(BF16)
    return hi, mid, lo


def _dot(a, b):
    return jnp.dot(a, b, preferred_element_type=F32)


def _dot_nt(a, b):
    return lax.dot_general(a, b, (((1,), (1,)), ((), ())), preferred_element_type=F32)


def _norm_matmul_kernel(x_ref, g_ref, w_ref, cs_ref, *rest, side):
    if side:
        wsh_ref, wsl_ref, o_ref, so_ref, h_sc = rest
    else:
        o_ref, h_sc = rest

    @pl.when(pl.program_id(1) == 0)
    def _():
        hn = _rms(x_ref[...], g_ref[...])
        h_hi = hn.astype(BF16)
        h_sc[...] = h_hi
        if side:
            h_lo = (hn - h_hi.astype(F32)).astype(BF16)
            so_ref[...] = (_dot(h_hi, wsh_ref[...]) + _dot(h_lo, wsh_ref[...])
                           + _dot(h_hi, wsl_ref[...]))

    o_ref[...] = (_dot(h_sc[...], w_ref[...]) * cs_ref[...]).astype(o_ref.dtype)


def norm_matmul(x, g, w, col_scale=None, side_w=None, tm=512, tn=2048):
    T, D = x.shape
    N = w.shape[1]
    tm, tn = _tile(T, tm), _tile(N, tn)
    if col_scale is None:
        col_scale = jnp.ones((1, N), F32)
    in_specs = [pl.BlockSpec((tm, D), lambda i, j: (i, 0)),
                pl.BlockSpec((1, D), lambda i, j: (0, 0)),
                pl.BlockSpec((D, tn), lambda i, j: (0, j)),
                pl.BlockSpec((1, tn), lambda i, j: (0, j))]
    args = [x, g.reshape(1, D), w, col_scale]
    out_shape = [jax.ShapeDtypeStruct((T, N), BF16)]
    out_specs = [pl.BlockSpec((tm, tn), lambda i, j: (i, j))]
    side = side_w is not None
    if side:
        ws_hi = side_w.astype(BF16)
        ws_lo = (side_w - ws_hi.astype(F32)).astype(BF16)
        ns = side_w.shape[1]
        in_specs += [pl.BlockSpec((D, ns), lambda i, j: (0, 0))] * 2
        args += [ws_hi, ws_lo]
        out_shape.append(jax.ShapeDtypeStruct((T, ns), F32))
        out_specs.append(pl.BlockSpec((tm, ns), lambda i, j: (i, 0)))
    res = pl.pallas_call(
        functools.partial(_norm_matmul_kernel, side=side),
        grid=(T // tm, N // tn),
        in_specs=in_specs, out_specs=out_specs, out_shape=out_shape,
        scratch_shapes=[pltpu.VMEM((tm, D), BF16)],
        compiler_params=_params("parallel", "arbitrary"),
        name="norm_matmul",
    )(*args)
    return res if side else res[0]


def _matmul_res_kernel(*refs, n):
    x_ref = refs[0]
    o_ref = refs[1 + 2 * n]
    acc = x_ref[...]
    for k in range(n):
        acc = acc + _dot(refs[1 + 2 * k][...], refs[2 + 2 * k][...])
    o_ref[...] = acc


def matmul_res(x, pairs, tm=1024, tn=1024):
    T, N = x.shape
    tm, tn = _tile(T, tm), _tile(N, tn)
    in_specs = [pl.BlockSpec((tm, tn), lambda i, j: (i, j))]
    args = [x]
    for a, w in pairs:
        K = a.shape[1]
        in_specs += [pl.BlockSpec((tm, K), lambda i, j: (i, 0)),
                     pl.BlockSpec((K, tn), lambda i, j: (0, j))]
        args += [a, w]
    return pl.pallas_call(
        functools.partial(_matmul_res_kernel, n=len(pairs)),
        grid=(T // tm, N // tn),
        in_specs=in_specs,
        out_specs=pl.BlockSpec((tm, tn), lambda i, j: (i, j)),
        out_shape=jax.ShapeDtypeStruct((T, N), F32),
        compiler_params=_params("parallel", "parallel"),
        name="matmul_res",
    )(*args)


def _mla_prep_kernel(z_ref, cos_ref, sin_ref, qg_ref, kvg_ref, wq_ref, wqs_ref, wk_ref, wv_ref,
                     q_ref, k_ref, v_ref, *, scale):
    cos2 = cos_ref[...]
    sin2 = sin_ref[...]
    cq = _rms(z_ref[:, 512:1280].astype(F32), qg_ref[...]).astype(BF16)
    qa = _dot(cq, wq_ref[...])
    qs = _dot(cq, wqs_ref[...])
    for h in range(MLA_HEADS):
        a, b = h * 256, h * 256 + 128
        q_ref[:, a:b] = (qa[:, a:b] * scale).astype(BF16)
        q_ref[:, b:b + 128] = ((qa[:, b:b + 128] * cos2 + qs[:, h * 128:(h + 1) * 128] * sin2)
                               * scale).astype(BF16)
    ckv = _rms(z_ref[:, 1280:1792].astype(F32), kvg_ref[...]).astype(BF16)
    kn = _dot(ckv, wk_ref[...])
    v_ref[...] = _dot(ckv, wv_ref[...]).astype(BF16)
    kpe = (z_ref[:, 1792:1920].astype(F32) * cos2 + z_ref[:, 1920:2048].astype(F32) * sin2).astype(BF16)
    for h in range(MLA_HEADS):
        k_ref[:, h * 256:h * 256 + 128] = kn[:, h * 128:(h + 1) * 128].astype(BF16)
        k_ref[:, h * 256 + 128:(h + 1) * 256] = kpe


def mla_prep(z, cos2, sin2, q_norm, kv_norm, wq_all, wq_swap, wk, wv, tm=512):
    T = z.shape[0]
    tm = _tile(T, tm)
    full = lambda a: pl.BlockSpec(a.shape, lambda i: (0, 0))
    qg, kvg = q_norm.reshape(1, -1), kv_norm.reshape(1, -1)
    return pl.pallas_call(
        functools.partial(_mla_prep_kernel, scale=LOG2E * (MLA_NOPE + MLA_ROPE) ** -0.5),
        grid=(T // tm,),
        in_specs=[pl.BlockSpec((tm, 2048), lambda i: (i, 1)),
                  pl.BlockSpec((tm, LANES), lambda i: (i, 0)),
                  pl.BlockSpec((tm, LANES), lambda i: (i, 0)),
                  full(qg), full(kvg), full(wq_all), full(wq_swap), full(wk), full(wv)],
        out_specs=[pl.BlockSpec((tm, MLA_HEADS * 256), lambda i: (i, 0)),
                   pl.BlockSpec((tm, MLA_HEADS * 256), lambda i: (i, 0)),
                   pl.BlockSpec((tm, MLA_HEADS * MLA_V), lambda i: (i, 0))],
        out_shape=[jax.ShapeDtypeStruct((T, MLA_HEADS * 256), BF16),
                   jax.ShapeDtypeStruct((T, MLA_HEADS * 256), BF16),
                   jax.ShapeDtypeStruct((T, MLA_HEADS * MLA_V), BF16)],
        compiler_params=_params("parallel"),
        name="mla_prep",
    )(z, cos2, sin2, qg, kvg, wq_all, wq_swap, wk, wv)


def _flash_kernel(*refs, tq, tk, groups, has_bias):
    if has_bias:
        q_ref, k_ref, v_ref, kb_ref, o_ref, m_sc, acc_sc, s_sc = refs
    else:
        q_ref, k_ref, v_ref, o_ref, m_sc, acc_sc, s_sc = refs
        kb_ref = None
    i = pl.program_id(2)
    rg = tq // groups
    dv = v_ref.shape[-1]
    m_sc[...] = jnp.full(m_sc.shape, NEG, F32)
    acc_sc[...] = jnp.zeros(acc_sc.shape, F32)
    ones = jnp.ones((tk, LANES), BF16)

    def scores(j, slot, diag=False):
        k = k_ref[pl.ds(pl.multiple_of(j * tk, tk), tk), :]
        kb = kb_ref[pl.ds(j, 1), :] if has_bias else None
        if not diag:
            s = _dot_nt(q_ref[...], k)
            s_sc[slot] = s - kb if has_bias else s
            return
        for g in range(groups):
            rows = slice(g * rg, (g + 1) * rg)
            kw = (g + 1) * rg
            s = _dot_nt(q_ref[rows, :], k[:kw])
            if has_bias:
                s = s - kb[:, :kw]
            s_sc[slot, rows, :kw] = s

    def consume(j, slot, diag):
        v1 = jnp.concatenate([v_ref[pl.ds(pl.multiple_of(j * tk, tk), tk), :], ones], axis=1)
        for g in range(groups):
            rows = slice(g * rg, (g + 1) * rg)
            kw = (g + 1) * rg if diag else tk
            s = s_sc[slot, rows, :kw]
            if diag:
                qpos = g * rg + lax.broadcasted_iota(jnp.int32, (rg, kw), 0)
                kpos = lax.broadcasted_iota(jnp.int32, (rg, kw), 1)
                s = jnp.where(kpos <= qpos, s, NEG)
            m_prev = m_sc[rows, :]
            m_new = jnp.maximum(m_prev, jnp.max(s, axis=-1, keepdims=True))
            alpha = jnp.exp2(m_prev - m_new)
            p = jnp.exp2((s - m_new).astype(BF16))
            acc_sc[rows, :] = alpha * acc_sc[rows, :] + _dot(p, v1[:kw])
            m_sc[rows, :] = m_new

    scores(i, 1, diag=True)

    @pl.when(i == 0)
    def _():
        consume(i, 1, True)

    @pl.when(i > 0)
    def _():
        scores(0, 0)
        consume(i, 1, True)

    def pair(jj, c):
        j = 2 * jj
        scores(j + 1, 1)
        consume(j, 0, False)
        scores(j + 2, 0)
        consume(j + 1, 1, False)
        return c

    n_pairs = jnp.maximum(i - 1, 0) // 2
    lax.fori_loop(0, n_pairs, pair, 0)
    r = 2 * n_pairs

    @pl.when(i - r == 1)
    def _():
        consume(r, 0, False)

    @pl.when(i - r == 2)
    def _():
        scores(r + 1, 1)
        consume(r, 0, False)
        consume(r + 1, 1, False)

    o_ref[...] = (acc_sc[:, :dv] / acc_sc[:, dv:]).astype(o_ref.dtype)


def flash_attention(q_arr, k_arr, v_arr, kbias, *, heads, dk, dv, q_off, k_off, v_off, tq, tk, groups=8):
    B, S = q_arr.shape[:2]
    assert S % tk == 0 and tk == tq and tq % groups == 0
    has_bias = kbias is not None
    in_specs = [pl.BlockSpec((None, tq, dk), lambda b, h, i: (b, i, q_off + h)),
                pl.BlockSpec((None, S, dk), lambda b, h, i: (b, 0, k_off + h)),
                pl.BlockSpec((None, S, dv), lambda b, h, i: (b, 0, v_off + h))]
    args = [q_arr, k_arr, v_arr]
    if has_bias:
        in_specs.append(pl.BlockSpec((None, None, S // tk, tk), lambda b, h, i: (b, h, 0, 0)))
        args.append(kbias)
    return pl.pallas_call(
        functools.partial(_flash_kernel, tq=tq, tk=tk, groups=groups, has_bias=has_bias),
        grid=(B, heads, S // tq),
        in_specs=in_specs,
        out_specs=pl.BlockSpec((None, tq, dv), lambda b, h, i: (b, i, h)),
        out_shape=jax.ShapeDtypeStruct((B, S, heads * dv), BF16),
        scratch_shapes=[pltpu.VMEM((tq, 1), F32), pltpu.VMEM((tq, dv + LANES), F32),
                        pltpu.VMEM((2, tq, tk), F32)],
        compiler_params=_params("parallel", "parallel", "arbitrary"),
        name="flash_attention",
    )(*args)


def _attn_tiles(S):
    tk = _tile(S, 1024)
    return tk, tk


def _swa_kernel(sink_ref, q_ref, kvc_ref, kvp_ref, bias_ref, o_ref):
    i = pl.program_id(1)
    lane = lax.broadcasted_iota(jnp.int32, (1, 2 * WINDOW), 1)
    pen = jnp.where((lane < WINDOW) & (i == 0), NEG, 0.0).astype(F32)
    for g in range(SWA_KV_HEADS):
        kcat = jnp.concatenate([kvp_ref[:, g * 128:(g + 1) * 128], kvc_ref[:, g * 128:(g + 1) * 128]], axis=0)
        vcat = jnp.concatenate([kvp_ref[:, 256 + g * 128:256 + (g + 1) * 128],
                                kvc_ref[:, 256 + g * 128:256 + (g + 1) * 128]], axis=0)
        for hh in range(SWA_GROUP):
            h = g * SWA_GROUP + hh
            s = _dot_nt(q_ref[:, h * 128:(h + 1) * 128], kcat) * (SWA_HEAD_DIM ** -0.5) + bias_ref[h] + pen
            sink = sink_ref[h]
            m = jnp.maximum(jnp.max(s, axis=-1, keepdims=True), sink)
            p = jnp.exp(s - m)
            denom = jnp.sum(p, axis=-1, keepdims=True) + jnp.exp(sink - m)
            o = _dot(p.astype(BF16), vcat) / denom
            o_ref[:, h * 128:(h + 1) * 128] = o.astype(o_ref.dtype)


def swa_attention(z, bias_tab, sinks, B, S):
    T = z.shape[0]
    nb = S // WINDOW
    return pl.pallas_call(
        _swa_kernel,
        grid=(B, nb),
        in_specs=[pl.BlockSpec(memory_space=pltpu.SMEM),
                  pl.BlockSpec((WINDOW, 2048), lambda b, i: (b * nb + i, 0)),
                  pl.BlockSpec((WINDOW, 512), lambda b, i: (b * nb + i, 4)),
                  pl.BlockSpec((WINDOW, 512), lambda b, i: (b * nb + jnp.maximum(i - 1, 0), 4)),
                  pl.BlockSpec(bias_tab.shape, lambda b, i: (0, 0, 0))],
        out_specs=pl.BlockSpec((WINDOW, 2048), lambda b, i: (b * nb + i, 0)),
        out_shape=jax.ShapeDtypeStruct((T, 2048), BF16),
        compiler_params=_params("parallel", "arbitrary"),
        name="swa_attention",
    )(sinks, z, z, z, bias_tab)


def _decay_kernel(f_ref, b_ref, o_ref, carry_sc, *, tm):
    @pl.when(pl.program_id(1) == 0)
    def _():
        carry_sc[...] = jnp.zeros(carry_sc.shape, F32)

    x = f_ref[...] + b_ref[...]
    lf = jnp.minimum(x, 0.0) - jnp.log1p(jnp.exp(-jnp.abs(x)))
    row = lax.broadcasted_iota(jnp.int32, (tm, tm), 0)
    col = lax.broadcasted_iota(jnp.int32, (tm, tm), 1)
    tri = jnp.where(col <= row, 1.0, 0.0).astype(BF16)
    hi, mid, lo = _split3(lf)
    cs = _dot(tri, hi) + _dot(tri, mid) + _dot(tri, lo) + carry_sc[...]
    carry_sc[...] = cs[tm - 1:tm, :]
    o_ref[...] = (cs * LOG2E).T


def decay_cumsum(f_logit, b_f, B, S, tm=512):
    tm = _tile(S, tm)
    ns = S // tm
    return pl.pallas_call(
        functools.partial(_decay_kernel, tm=tm),
        grid=(B, ns),
        in_specs=[pl.BlockSpec((tm, LANES), lambda b, i: (b * ns + i, 0)),
                  pl.BlockSpec((1, LANES), lambda b, i: (0, 0))],
        out_specs=pl.BlockSpec((None, LANES, tm), lambda b, i: (b, 0, i)),
        out_shape=jax.ShapeDtypeStruct((B, LANES, S), F32),
        scratch_shapes=[pltpu.VMEM((1, LANES), F32)],
        compiler_params=_params("parallel", "arbitrary"),
        name="decay_cumsum",
    )(f_logit, b_f)


def _cross_kernel(x_ref, g_ref, wq_ref, k_ref, v_ref, wo_ref, o_ref):
    x = x_ref[...]
    hn = _rms(x, g_ref[...]).astype(BF16)
    q = (_dot(hn, wq_ref[...]) * (X_HEAD_DIM ** -0.5)).astype(BF16)
    outs = []
    for h in range(X_HEADS):
        sl = slice(h * X_HEAD_DIM, (h + 1) * X_HEAD_DIM)
        s = _dot_nt(q[:, sl], k_ref[:, sl])
        m = jnp.max(s, axis=-1, keepdims=True)
        p = jnp.exp(s - m)
        o = _dot(p.astype(BF16), v_ref[:, sl]) / jnp.sum(p, axis=-1, keepdims=True)
        outs.append(o.astype(BF16))
    o_ref[...] = x + _dot(jnp.concatenate(outs, axis=1), wo_ref[...])


def cross_attention(x, g, wq, kv, wo, B, S, tm=512):
    T, D = x.shape
    M = kv.shape[1]
    XW = X_HEADS * X_HEAD_DIM
    tm = _tile(S, tm)
    ns = S // tm
    return pl.pallas_call(
        _cross_kernel,
        grid=(B, ns),
        in_specs=[pl.BlockSpec((tm, D), lambda b, i: (b * ns + i, 0)),
                  pl.BlockSpec((1, D), lambda b, i: (0, 0)),
                  pl.BlockSpec((D, XW), lambda b, i: (0, 0)),
                  pl.BlockSpec((None, M, XW), lambda b, i: (b, 0, 0)),
                  pl.BlockSpec((None, M, XW), lambda b, i: (b, 0, 1)),
                  pl.BlockSpec((XW, D), lambda b, i: (0, 0))],
        out_specs=pl.BlockSpec((tm, D), lambda b, i: (b * ns + i, 0)),
        out_shape=jax.ShapeDtypeStruct((T, D), F32),
        compiler_params=_params("parallel", "parallel"),
        name="cross_attention",
    )(x, g.reshape(1, D), wq, kv, kv, wo)


def _ffn_kernel(x_ref, g_ref, wg_ref, wu_ref, wd_ref, o_ref, h_sc):
    @pl.when(pl.program_id(1) == 0)
    def _():
        x = x_ref[...]
        h_sc[...] = _rms(x, g_ref[...]).astype(BF16)
        o_ref[...] = x

    h = h_sc[...]
    gate = _dot(h, wg_ref[...])
    up = _dot(h, wu_ref[...])
    a = (gate * jax.nn.sigmoid(gate) * up).astype(BF16)
    o_ref[...] += _dot(a, wd_ref[...])


def ffn(x, g, wg, wu, wd, tm=512, tf=1024):
    T, D = x.shape
    F = wg.shape[1]
    tm, tf = _tile(T, tm), _tile(F, tf)
    return pl.pallas_call(
        _ffn_kernel,
        grid=(T // tm, F // tf),
        in_specs=[pl.BlockSpec((tm, D), lambda i, f: (i, 0)),
                  pl.BlockSpec((1, D), lambda i, f: (0, 0)),
                  pl.BlockSpec((D, tf), lambda i, f: (0, f)),
                  pl.BlockSpec((D, tf), lambda i, f: (0, f)),
                  pl.BlockSpec((tf, D), lambda i, f: (f, 0))],
        out_specs=pl.BlockSpec((tm, D), lambda i, f: (i, 0)),
        out_shape=jax.ShapeDtypeStruct((T, D), F32),
        scratch_shapes=[pltpu.VMEM((tm, D), BF16)],
        compiler_params=_params("parallel", "arbitrary"),
        name="ffn",
    )(x, g.reshape(1, D), wg, wu, wd)


def _router_kernel(x_ref, g_ref, wh_ref, wl_ref, b_ref, mi_ref, mg_ref, cnt_ref, carry_sc, *, tm):
    @pl.when(pl.program_id(0) == 0)
    def _():
        carry_sc[...] = jnp.zeros(carry_sc.shape, F32)

    hn = _rms(x_ref[...], g_ref[...])
    h_hi = hn.astype(BF16)
    h_lo = (hn - h_hi.astype(F32)).astype(BF16)
    logits = (_dot(h_hi, wh_ref[...]) + _dot(h_lo, wh_ref[...]) + _dot(h_hi, wl_ref[...])) + b_ref[...]
    lane = lax.broadcasted_iota(jnp.int32, logits.shape, 1)
    logits = jnp.where(lane < N_EXPERTS, logits, -jnp.inf)
    m1 = jnp.max(logits, axis=-1, keepdims=True)
    i1 = jnp.min(jnp.where(logits == m1, lane, LANES), axis=-1, keepdims=True)
    oh1 = lane == i1
    l2 = jnp.where(oh1, -jnp.inf, logits)
    m2 = jnp.max(l2, axis=-1, keepdims=True)
    i2 = jnp.min(jnp.where(l2 == m2, lane, LANES), axis=-1, keepdims=True)
    oh2 = lane == i2
    e2 = jnp.exp(m2 - m1)
    g1 = 1.0 / (1.0 + e2)
    mg_ref[...] = jnp.where(lane == 0, g1, jnp.where(lane == 1, e2 * g1, 0.0))
    row = lax.broadcasted_iota(jnp.int32, (tm, tm), 0)
    col = lax.broadcasted_iota(jnp.int32, (tm, tm), 1)
    tri = jnp.where(col < row, 1.0, 0.0).astype(BF16)
    oh1f = jnp.where(oh1, 1.0, 0.0)
    oh2f = jnp.where(oh2, 1.0, 0.0)
    ex1 = _dot(tri, oh1f.astype(BF16))
    ex2 = _dot(tri, oh2f.astype(BF16))
    tot1 = jnp.sum(oh1f, axis=0, keepdims=True)
    tot2 = jnp.sum(oh2f, axis=0, keepdims=True)
    carry = carry_sc[...]
    r1 = jnp.sum(jnp.where(oh1, ex1 + carry, 0.0), axis=-1, keepdims=True).astype(jnp.int32)
    r2 = jnp.sum(jnp.where(oh2, ex2 + (carry + tot1), 0.0), axis=-1, keepdims=True).astype(jnp.int32)
    mi_ref[...] = jnp.where(lane == 0, i1, jnp.where(lane == 1, i2, jnp.where(lane == 2, r1,
                            jnp.where(lane == 3, r2, 0))))
    total = carry + tot1 + tot2
    carry_sc[...] = total
    cnt_ref[...] = total


def router(x, g, w_router, b_router, tm=512):
    T, D = x.shape
    tm = _tile(T, tm)
    wp = jnp.pad(w_router, ((0, 0), (0, LANES - N_EXPERTS)))
    w_hi = wp.astype(BF16)
    w_lo = (wp - w_hi.astype(F32)).astype(BF16)
    bp = jnp.pad(b_router, (0, LANES - N_EXPERTS)).reshape(1, LANES)
    return pl.pallas_call(
        functools.partial(_router_kernel, tm=tm),
        grid=(T // tm,),
        in_specs=[pl.BlockSpec((tm, D), lambda i: (i, 0)),
                  pl.BlockSpec((1, D), lambda i: (0, 0)),
                  pl.BlockSpec((D, LANES), lambda i: (0, 0)),
                  pl.BlockSpec((D, LANES), lambda i: (0, 0)),
                  pl.BlockSpec((1, LANES), lambda i: (0, 0))],
        out_specs=[pl.BlockSpec((tm, LANES), lambda i: (i, 0)),
                   pl.BlockSpec((tm, LANES), lambda i: (i, 0)),
                   pl.BlockSpec((1, LANES), lambda i: (0, 0))],
        out_shape=[jax.ShapeDtypeStruct((T, LANES), jnp.int32),
                   jax.ShapeDtypeStruct((T, LANES), F32),
                   jax.ShapeDtypeStruct((1, LANES), F32)],
        scratch_shapes=[pltpu.VMEM((1, LANES), F32)],
        compiler_params=_params("arbitrary"),
        name="router",
    )(x, g.reshape(1, D), w_hi, w_lo, bp)


def _row_copy(src_ref, src_row, dst_ref, dst_row, sem):
    return pltpu.make_async_copy(src_ref.at[pl.ds(src_row, 1), :], dst_ref.at[pl.ds(dst_row, 1), :], sem)


def _dispatch_kernel(grp_ref, pos_ref, x_ref, xs_ref, zero_sc, sem, *, tm):
    def issue(r, c):
        _row_copy(x_ref, r, xs_ref, pos_ref[0, 2 * r], sem).start()
        _row_copy(x_ref, r, xs_ref, pos_ref[0, 2 * r + 1], sem).start()
        return c

    lax.fori_loop(0, tm, issue, 0, unroll=8)

    def pad_rows(fn):
        for e in range(N_EXPERTS):
            cnt, padded, off = grp_ref[e], grp_ref[N_EXPERTS + e], grp_ref[2 * N_EXPERTS + e]
            lax.fori_loop(cnt, padded, lambda r, c: fn(off + r, c), 0)
        lax.fori_loop(grp_ref[3 * N_EXPERTS], xs_ref.shape[0], fn, 0)

    @pl.when(pl.program_id(0) == 0)
    def _():
        zero_sc[...] = jnp.zeros(zero_sc.shape, F32)

        def start(p, c):
            _row_copy(zero_sc, 0, xs_ref, p, sem).start()
            return c

        def wait(p, c):
            _row_copy(zero_sc, 0, xs_ref, p, sem).wait()
            return c

        pad_rows(start)
        pad_rows(wait)

    def drain(r, c):
        _row_copy(x_ref, r, xs_ref, pos_ref[0, 2 * r], sem).wait()
        _row_copy(x_ref, r, xs_ref, pos_ref[0, 2 * r + 1], sem).wait()
        return c

    lax.fori_loop(0, tm, drain, 0, unroll=8)


def moe_dispatch(x, pos, grp, n_rows, tm=512):
    T, D = x.shape
    tm = _tile(T, tm)
    pos3 = pos.reshape(T // tm, 1, 2 * tm)
    return pl.pallas_call(
        functools.partial(_dispatch_kernel, tm=tm),
        grid_spec=pltpu.PrefetchScalarGridSpec(
            num_scalar_prefetch=1,
            grid=(T // tm,),
            in_specs=[pl.BlockSpec((None, 1, 2 * tm), lambda i, grp: (i, 0, 0), memory_space=pltpu.SMEM),
                      pl.BlockSpec((tm, D), lambda i, grp: (i, 0))],
            out_specs=pl.BlockSpec(memory_space=pl.ANY),
            scratch_shapes=[pltpu.VMEM((8, D), F32), pltpu.SemaphoreType.DMA(())]),
        out_shape=jax.ShapeDtypeStruct((n_rows, D), F32),
        compiler_params=_params("arbitrary"),
        name="moe_dispatch",
    )(grp, pos3, x)


def _moe_expert_kernel(te_ref, nact_ref, xs_ref, g_ref, wg_ref, wu_ref, wd_ref, y_ref, h_sc):
    i, f = pl.program_id(0), pl.program_id(1)
    active = i < nact_ref[0]

    @pl.when(active & (f == 0))
    def _():
        h_sc[...] = _rms(xs_ref[...], g_ref[...]).astype(BF16)

    @pl.when(active)
    def _():
        h = h_sc[...]
        gate = _dot(h, wg_ref[...])
        up = _dot(h, wu_ref[...])
        a = (gate * jax.nn.sigmoid(gate) * up).astype(BF16)
        contrib = _dot(a, wd_ref[...])

        @pl.when(f == 0)
        def _():
            y_ref[...] = contrib

        @pl.when(f != 0)
        def _():
            y_ref[...] += contrib

    @pl.when(jnp.logical_not(active) & (f == 0))
    def _():
        y_ref[...] = jnp.zeros(y_ref.shape, F32)


def moe_experts(xs, g, tile_expert, n_active, wg, wu, wd, tm, tf=1024):
    P, D = xs.shape
    F = wg.shape[2]
    tf = _tile(F, tf)
    nf = F // tf
    row = lambda i, f, te, na: (jnp.minimum(i, na[0] - 1), 0)
    fcol = lambda i, f, na: jnp.where(i < na[0], f, nf - 1)
    return pl.pallas_call(
        _moe_expert_kernel,
        grid_spec=pltpu.PrefetchScalarGridSpec(
            num_scalar_prefetch=2,
            grid=(P // tm, nf),
            in_specs=[pl.BlockSpec((tm, D), row),
                      pl.BlockSpec((1, D), lambda i, f, te, na: (0, 0)),
                      pl.BlockSpec((None, D, tf), lambda i, f, te, na: (te[i], 0, fcol(i, f, na))),
                      pl.BlockSpec((None, D, tf), lambda i, f, te, na: (te[i], 0, fcol(i, f, na))),
                      pl.BlockSpec((None, tf, D), lambda i, f, te, na: (te[i], fcol(i, f, na), 0))],
            out_specs=pl.BlockSpec((tm, D), lambda i, f, te, na: (i, 0)),
            scratch_shapes=[pltpu.VMEM((tm, D), BF16)]),
        out_shape=jax.ShapeDtypeStruct((P, D), F32),
        compiler_params=_params("arbitrary", "arbitrary"),
        name="moe_experts",
    )(tile_expert, n_active, xs, g.reshape(1, D), wg, wu, wd)


def _combine_kernel(pos_ref, x_ref, mg_ref, gf_ref, y_ref, o_ref, ybuf, sem, *, tm):
    def issue(r, c):
        _row_copy(y_ref, pos_ref[0, 2 * r], ybuf.at[0], r, sem).start()
        _row_copy(y_ref, pos_ref[0, 2 * r + 1], ybuf.at[1], r, sem).start()
        return c

    def drain(r, c):
        _row_copy(y_ref, pos_ref[0, 2 * r], ybuf.at[0], r, sem).wait()
        _row_copy(y_ref, pos_ref[0, 2 * r + 1], ybuf.at[1], r, sem).wait()
        return c

    lax.fori_loop(0, tm, issue, 0, unroll=8)
    lax.fori_loop(0, tm, drain, 0, unroll=8)
    mg = mg_ref[...]
    v = x_ref[...] + mg[:, 0:1] * ybuf[0] + mg[:, 1:2] * ybuf[1]
    o_ref[...] = _rms(v, gf_ref[...])


def moe_combine_norm(x, y, pos, gates, final_g, tm=512):
    T, D = x.shape
    tm = _tile(T, tm)
    pos3 = pos.reshape(T // tm, 1, 2 * tm)
    return pl.pallas_call(
        functools.partial(_combine_kernel, tm=tm),
        grid=(T // tm,),
        in_specs=[pl.BlockSpec((None, 1, 2 * tm), lambda i: (i, 0, 0), memory_space=pltpu.SMEM),
                  pl.BlockSpec((tm, D), lambda i: (i, 0)),
                  pl.BlockSpec((tm, LANES), lambda i: (i, 0)),
                  pl.BlockSpec((1, D), lambda i: (0, 0)),
                  pl.BlockSpec(memory_space=pl.ANY)],
        out_specs=pl.BlockSpec((tm, D), lambda i: (i, 0)),
        out_shape=jax.ShapeDtypeStruct((T, D), F32),
        scratch_shapes=[pltpu.VMEM((2, tm, D), F32), pltpu.SemaphoreType.DMA(())],
        compiler_params=_params("arbitrary"),
        name="moe_combine_norm",
    )(pos3, x, gates, final_g.reshape(1, D), y)


def routed_moe_norm(x, g, w_router, b_router, wg, wu, wd, final_g, tm_e=512):
    T, D = x.shape
    tm_e = _tile(T, tm_e)
    mi, mg, cnt = router(x, g, w_router, b_router)
    cnt = cnt[0, :N_EXPERTS].astype(jnp.int32)
    padded = (cnt + tm_e - 1) // tm_e * tm_e
    ends = jnp.cumsum(padded)
    off = ends - padded
    first = lambda e: jnp.sum(jnp.where(e[:, None] == jnp.arange(N_EXPERTS)[None, :], off[None, :], 0), axis=1)
    pos = jnp.stack([first(mi[:, 0]) + mi[:, 2], first(mi[:, 1]) + mi[:, 3]], axis=1)
    n_rows = 2 * T + N_EXPERTS * tm_e
    n_active = (ends[-1:] // tm_e).astype(jnp.int32)
    starts = jnp.minimum(jnp.arange(n_rows // tm_e, dtype=jnp.int32), n_active - 1) * tm_e
    tile_expert = jnp.sum(starts[:, None] >= ends[None, :], axis=1).astype(jnp.int32)
    grp = jnp.concatenate([cnt, padded, off, ends[-1:]]).astype(jnp.int32)
    xs = moe_dispatch(x, pos, grp, n_rows)
    y = moe_experts(xs, g, tile_expert, n_active, wg, wu, wd, tm_e)
    return moe_combine_norm(x, y, pos, mg, final_g)


def _pad_heads(w, heads, dim, to=LANES):
    lead = w.shape[:-1]
    w = w.reshape(lead + (heads, dim))
    w = jnp.pad(w, [(0, 0)] * len(lead) + [(0, 0), (0, to - dim)])
    return w.reshape(lead + (heads * to,))


def _swap_halves(w):
    half = w.shape[-1] // 2
    return jnp.concatenate([w[..., half:], w[..., :half]], axis=-1)


def _even_w_in(w):
    c_q = w[:, :MLA_Q_RANK]
    o = MLA_Q_RANK
    c_kv = w[:, o:o + MLA_KV_RANK]
    o += MLA_KV_RANK
    k_rope = w[:, o:o + MLA_ROPE]
    o += MLA_ROPE
    q_s = w[:, o:o + SWA_HEADS * SWA_HEAD_DIM]
    o += SWA_HEADS * SWA_HEAD_DIM
    k_s = w[:, o:o + SWA_KV_HEADS * SWA_HEAD_DIM]
    o += SWA_KV_HEADS * SWA_HEAD_DIM
    v_s = w[:, o:o + SWA_KV_HEADS * SWA_HEAD_DIM]
    pad = lambda a: jnp.pad(a, ((0, 0), (0, LANES - a.shape[1])))
    return jnp.concatenate([
        _pad_heads(q_s, SWA_HEADS, SWA_HEAD_DIM),
        _pad_heads(k_s, SWA_KV_HEADS, SWA_HEAD_DIM),
        _pad_heads(v_s, SWA_KV_HEADS, SWA_HEAD_DIM),
        c_q, c_kv, pad(k_rope), pad(_swap_halves(k_rope))], axis=1).astype(BF16)


def _t5_bucket(dist):
    max_exact = REL_BUCKETS // 2
    large = max_exact + (jnp.log(jnp.maximum(dist, 1).astype(F32) / max_exact)
                         / math.log(REL_MAX_DIST / max_exact) * (REL_BUCKETS - max_exact)).astype(jnp.int32)
    large = jnp.minimum(large, REL_BUCKETS - 1)
    return jnp.where(dist < max_exact, dist, large)


def _swa_bias_table(rel_bias):
    i = jnp.arange(WINDOW)[:, None]
    j = jnp.arange(2 * WINDOW)[None, :]
    rel = WINDOW + i - j
    onehot = (_t5_bucket(jnp.maximum(rel, 0))[..., None] == jnp.arange(REL_BUCKETS)).astype(F32)
    bias = jnp.einsum('ijb,bh->ijh', onehot, rel_bias.astype(F32), precision=lax.Precision.HIGHEST)
    valid = (rel >= 0) & (rel < WINDOW)
    return jnp.where(valid[None], jnp.transpose(bias, (2, 0, 1)), NEG)


def _rope_tables(positions):
    half = MLA_ROPE // 2
    inv = ROPE_THETA ** (-jnp.arange(half, dtype=F32) / half)
    ang = positions.astype(F32).reshape(-1, 1) * inv
    cos, sin = jnp.cos(ang), jnp.sin(ang)
    z = jnp.zeros((ang.shape[0], LANES - MLA_ROPE), F32)
    return jnp.concatenate([cos, cos, z], axis=1), jnp.concatenate([-sin, sin, z], axis=1)


def kernel(x, mem, positions, rel_bias, norm_mix, norm_cross, norm_mem, norm_ffn, x_wq, x_wk, x_wv, x_wo,
           ev_w_in, ev_mla_q_norm, ev_mla_w_uq, ev_mla_kv_norm, ev_mla_w_ukv, ev_swa_sinks, ev_w_o,
           ev_ffn_w_gate, ev_ffn_w_up, ev_ffn_w_down, od_w_in, od_fox_b_f, od_w_o, od_router_w, od_router_b,
           od_moe_w_gate, od_moe_w_up, od_moe_w_down, final_norm_g):
    B, S, D = x.shape
    T = B * S
    M = mem.shape[1]
    xs = x.reshape(T, D)
    mem2 = mem.reshape(B * M, D)
    bf = lambda a: a.astype(BF16)

    def cross(xs, layer):
        kvw = bf(jnp.concatenate([x_wk[layer], x_wv[layer]], axis=1))
        kv = norm_matmul(mem2, norm_mem[layer], kvw, tm=256, tn=1024).reshape(B, M, -1)
        return cross_attention(xs, norm_cross[layer], bf(x_wq[layer]), kv, bf(x_wo[layer]), B, S)

    z0 = norm_matmul(xs, norm_mix[0], _even_w_in(ev_w_in[0]), tm=512, tn=2048)
    cos2, sin2 = _rope_tables(positions)
    w_uq = ev_mla_w_uq[0].reshape(MLA_Q_RANK, MLA_HEADS, MLA_NOPE + MLA_ROPE)
    rope_w = w_uq[..., MLA_NOPE:]
    padr = lambda a: jnp.pad(a, ((0, 0), (0, 0), (0, LANES - MLA_ROPE)))
    wq_all = bf(jnp.concatenate([w_uq[..., :MLA_NOPE], padr(rope_w)], axis=-1).reshape(MLA_Q_RANK, -1))
    wq_swap = bf(padr(_swap_halves(rope_w)).reshape(MLA_Q_RANK, -1))
    w_ukv = ev_mla_w_ukv[0].reshape(MLA_KV_RANK, MLA_HEADS, MLA_NOPE + MLA_V)
    wk = bf(w_ukv[..., :MLA_NOPE].reshape(MLA_KV_RANK, -1))
    wv = bf(w_ukv[..., MLA_NOPE:].reshape(MLA_KV_RANK, -1))
    q, k, v = mla_prep(z0, cos2, sin2, ev_mla_q_norm[0], ev_mla_kv_norm[0], wq_all, wq_swap, wk, wv)
    tq, tk = _attn_tiles(S)
    o_mla = flash_attention(q.reshape(B, S, -1), k.reshape(B, S, -1), v.reshape(B, S, -1), None,
                            heads=MLA_HEADS, dk=256, dv=MLA_V, q_off=0, k_off=0, v_off=0, tq=tq, tk=tk)
    o_swa = swa_attention(z0, _swa_bias_table(rel_bias), ev_swa_sinks[0], B, S)
    n_mla = MLA_HEADS * MLA_V
    w_o_swa = ev_w_o[0][n_mla:].reshape(SWA_HEADS, SWA_HEAD_DIM, D)
    w_o_swa = jnp.pad(w_o_swa, ((0, 0), (0, LANES - SWA_HEAD_DIM), (0, 0))).reshape(SWA_HEADS * LANES, D)
    xs = matmul_res(xs, [(o_mla.reshape(T, n_mla), bf(ev_w_o[0][:n_mla])), (o_swa, bf(w_o_swa))])
    xs = cross(xs, 0)
    xs = ffn(xs, norm_ffn[0], bf(ev_ffn_w_gate[0]), bf(ev_ffn_w_up[0]), bf(ev_ffn_w_down[0]))

    FW = FOX_HEADS * FOX_HEAD_DIM
    col_scale = jnp.concatenate([jnp.full((1, FW), LOG2E * FOX_HEAD_DIM ** -0.5, F32), jnp.ones((1, 2 * FW), F32)], axis=1)
    w_f = jnp.pad(od_w_in[0][:, 3 * FW:], ((0, 0), (0, LANES - FOX_HEADS)))
    z1, f_logit = norm_matmul(xs, norm_mix[1], bf(od_w_in[0][:, :3 * FW]), col_scale=col_scale, side_w=w_f,
                              tm=512, tn=2048)
    b_f = jnp.pad(od_fox_b_f[0], (0, LANES - FOX_HEADS)).reshape(1, LANES)
    decay = decay_cumsum(f_logit, b_f, B, S)[:, :FOX_HEADS, :].reshape(B, FOX_HEADS, S // tk, tk)
    z1 = z1.reshape(B, S, 3 * FW)
    o_fox = flash_attention(z1, z1, z1, decay, heads=FOX_HEADS, dk=FOX_HEAD_DIM, dv=FOX_HEAD_DIM,
                            q_off=0, k_off=FOX_HEADS, v_off=2 * FOX_HEADS, tq=tq, tk=tk)
    xs = matmul_res(xs, [(o_fox.reshape(T, FW), bf(od_w_o[0]))])
    xs = cross(xs, 1)
    out = routed_moe_norm(xs, norm_ffn[1], od_router_w[0], od_router_b[0], bf(od_moe_w_gate[0]),
                          bf(od_moe_w_up[0]), bf(od_moe_w_down[0]), final_norm_g)
    return out.reshape(B, S, D)
```

```python
import functools
import math

import jax
import jax.numpy as jnp
from jax import lax
from jax.experimental import pallas as pl
from jax.experimental.pallas import tpu as pltpu

F32 = jnp.float32
BF16 = jnp.bfloat16

EPS = 1e-6
NEG = -1e30
LANES = 128
LOG2E = math.log2(math.e)

MLA_HEADS = 8
MLA_Q_RANK = 768
MLA_KV_RANK = 512
MLA_NOPE = 128
MLA_ROPE = 64
MLA_V = 128
ROPE_THETA = 10000.0
SWA_HEADS = 16
SWA_KV_HEADS = 2
SWA_GROUP = SWA_HEADS // SWA_KV_HEADS
SWA_HEAD_DIM = 64
WINDOW = 128
REL_BUCKETS = 32
REL_MAX_DIST = 128
FOX_HEADS = 16
FOX_HEAD_DIM = 128
X_HEADS = 4
X_HEAD_DIM = 128
N_EXPERTS = 8

VMEM_LIMIT_BYTES = 56 * 1024 * 1024


def _tile(n, pref):
    return pref if n % pref == 0 else n


def _params(*sem, flags=None):
    return pltpu.CompilerParams(dimension_semantics=sem, vmem_limit_bytes=VMEM_LIMIT_BYTES, flags=flags)


def _rms(xf, g):
    ms = jnp.mean(xf * xf, axis=-1, keepdims=True)
    return xf * lax.rsqrt(ms + EPS) * g


def _split3(v):
    hi = v.astype(BF16)
    r = v - hi.astype(F32)
    mid = r.astype(BF16)
    lo = (r - mid.astype(F32)).astype(BF16)
    return hi, mid, lo


def _dot(a, b):
    return jnp.dot(a, b, preferred_element_type=F32)


def _dot_nt(a, b):
    return lax.dot_general(a, b, (((1,), (1,)), ((), ())), preferred_element_type=F32)


def _norm_matmul_kernel(x_ref, g_ref, w_ref, cs_ref, *rest, side):
    if side:
        wsh_ref, wsl_ref, o_ref, so_ref, h_sc = rest
    else:
        o_ref, h_sc = rest

    @pl.when(pl.program_id(1) == 0)
    def _():
        hn = _rms(x_ref[...], g_ref[...])
        h_hi = hn.astype(BF16)
        h_sc[...] = h_hi
        if side:
            h_lo = (hn - h_hi.astype(F32)).astype(BF16)
            so_ref[...] = (_dot(h_hi, wsh_ref[...]) + _dot(h_lo, wsh_ref[...])
                           + _dot(h_hi, wsl_ref[...]))

    o_ref[...] = (_dot(h_sc[...], w_ref[...]) * cs_ref[...]).astype(o_ref.dtype)


def norm_matmul(x, g, w, col_scale=None, side_w=None, tm=512, tn=2048):
    T, D = x.shape
    N = w.shape[1]
    tm, tn = _tile(T, tm), _tile(N, tn)
    if col_scale is None:
        col_scale = jnp.ones((1, N), F32)
    in_specs = [pl.BlockSpec((tm, D), lambda i, j: (i, 0)),
                pl.BlockSpec((1, D), lambda i, j: (0, 0)),
                pl.BlockSpec((D, tn), lambda i, j: (0, j)),
                pl.BlockSpec((1, tn), lambda i, j: (0, j))]
    args = [x, g.reshape(1, D), w, col_scale]
    out_shape = [jax.ShapeDtypeStruct((T, N), BF16)]
    out_specs = [pl.BlockSpec((tm, tn), lambda i, j: (i, j))]
    side = side_w is not None
    if side:
        ws_hi = side_w.astype(BF16)
        ws_lo = (side_w - ws_hi.astype(F32)).astype(BF16)
        ns = side_w.shape[1]
        in_specs += [pl.BlockSpec((D, ns), lambda i, j: (0, 0))] * 2
        args += [ws_hi, ws_lo]
        out_shape.append(jax.ShapeDtypeStruct((T, ns), F32))
        out_specs.append(pl.BlockSpec((tm, ns), lambda i, j: (i, 0)))
    res = pl.pallas_call(
        functools.partial(_norm_matmul_kernel, side=side),
        grid=(T // tm, N // tn),
        in_specs=in_specs, out_specs=out_specs, out_shape=out_shape,
        scratch_shapes=[pltpu.VMEM((tm, D), BF16)],
        compiler_params=_params("parallel", "arbitrary"),
        name="norm_matmul",
    )(*args)
    return res if side else res[0]


def _matmul_res_kernel(*refs, n):
    x_ref = refs[0]
    o_ref = refs[1 + 2 * n]
    acc = x_ref[...]
    for k in range(n):
        acc = acc + _dot(refs[1 + 2 * k][...], refs[2 + 2 * k][...])
    o_ref[...] = acc


def matmul_res(x, pairs, tm=1024, tn=1024):
    T, N = x.shape
    tm, tn = _tile(T, tm), _tile(N, tn)
    in_specs = [pl.BlockSpec((tm, tn), lambda i, j: (i, j))]
    args = [x]
    for a, w in pairs:
        K = a.shape[1]
        in_specs += [pl.BlockSpec((tm, K), lambda i, j: (i, 0)),
                     pl.BlockSpec((K, tn), lambda i, j: (0, j))]
        args += [a, w]
    return pl.pallas_call(
        functools.partial(_matmul_res_kernel, n=len(pairs)),
        grid=(T // tm, N // tn),
        in_specs=in_specs,
        out_specs=pl.BlockSpec((tm, tn), lambda i, j: (i, j)),
        out_shape=jax.ShapeDtypeStruct((T, N), F32),
        compiler_params=_params("parallel", "parallel"),
        name="matmul_res",
    )(*args)


def _mla_prep_kernel(z_ref, cos_ref, sin_ref, qg_ref, kvg_ref, wq_ref, wqs_ref, wk_ref, wv_ref,
                     q_ref, k_ref, v_ref, *, scale):
    cos2 = cos_ref[...]
    sin2 = sin_ref[...]
    cq = _rms(z_ref[:, 512:1280].astype(F32), qg_ref[...]).astype(BF16)
    qa = _dot(cq, wq_ref[...])
    qs = _dot(cq, wqs_ref[...])
    for h in range(MLA_HEADS):
        a, b = h * 256, h * 256 + 128
        q_ref[:, a:b] = (qa[:, a:b] * scale).astype(BF16)
        q_ref[:, b:b + 128] = ((qa[:, b:b + 128] * cos2 + qs[:, h * 128:(h + 1) * 128] * sin2)
                               * scale).astype(BF16)
    ckv = _rms(z_ref[:, 1280:1792].astype(F32), kvg_ref[...]).astype(BF16)
    kn = _dot(ckv, wk_ref[...])
    v_ref[...] = _dot(ckv, wv_ref[...]).astype(BF16)
    kpe = (z_ref[:, 1792:1920].astype(F32) * cos2 + z_ref[:, 1920:2048].astype(F32) * sin2).astype(BF16)
    for h in range(MLA_HEADS):
        k_ref[:, h * 256:h * 256 + 128] = kn[:, h * 128:(h + 1) * 128].astype(BF16)
        k_ref[:, h * 256 + 128:(h + 1) * 256] = kpe


def mla_prep(z, cos2, sin2, q_norm, kv_norm, wq_all, wq_swap, wk, wv, tm=512):
    T = z.shape[0]
    tm = _tile(T, tm)
    full = lambda a: pl.BlockSpec(a.shape, lambda i: (0, 0))
    qg, kvg = q_norm.reshape(1, -1), kv_norm.reshape(1, -1)
    return pl.pallas_call(
        functools.partial(_mla_prep_kernel, scale=LOG2E * (MLA_NOPE + MLA_ROPE) ** -0.5),
        grid=(T // tm,),
        in_specs=[pl.BlockSpec((tm, 2048), lambda i: (i, 1)),
                  pl.BlockSpec((tm, LANES), lambda i: (i, 0)),
                  pl.BlockSpec((tm, LANES), lambda i: (i, 0)),
                  full(qg), full(kvg), full(wq_all), full(wq_swap), full(wk), full(wv)],
        out_specs=[pl.BlockSpec((tm, MLA_HEADS * 256), lambda i: (i, 0)),
                   pl.BlockSpec((tm, MLA_HEADS * 256), lambda i: (i, 0)),
                   pl.BlockSpec((tm, MLA_HEADS * MLA_V), lambda i: (i, 0))],
        out_shape=[jax.ShapeDtypeStruct((T, MLA_HEADS * 256), BF16),
                   jax.ShapeDtypeStruct((T, MLA_HEADS * 256), BF16),
                   jax.ShapeDtypeStruct((T, MLA_HEADS * MLA_V), BF16)],
        compiler_params=_params("parallel"),
        name="mla_prep",
    )(z, cos2, sin2, qg, kvg, wq_all, wq_swap, wk, wv)


def _flash_kernel(*refs, tq, tk, groups, has_bias):
    if has_bias:
        q_ref, k_ref, v_ref, kb_ref, o_ref, m_sc, acc_sc, s_sc = refs
    else:
        q_ref, k_ref, v_ref, o_ref, m_sc, acc_sc, s_sc = refs
        kb_ref = None
    i = pl.program_id(2)
    rg = tq // groups
    dv = v_ref.shape[-1]
    m_sc[...] = jnp.full(m_sc.shape, NEG, F32)
    acc_sc[...] = jnp.zeros(acc_sc.shape, F32)
    ones = jnp.ones((tk, LANES), BF16)

    def scores(j, slot, diag=False):
        k = k_ref[pl.ds(pl.multiple_of(j * tk, tk), tk), :]
        kb = kb_ref[pl.ds(j, 1), :] if has_bias else None
        if not diag:
            s = _dot_nt(q_ref[...], k)
            s_sc[slot] = s - kb if has_bias else s
            return
        for g in range(groups):
            rows = slice(g * rg, (g + 1) * rg)
            kw = (g + 1) * rg
            s = _dot_nt(q_ref[rows, :], k[:kw])
            if has_bias:
                s = s - kb[:, :kw]
            s_sc[slot, rows, :kw] = s

    def consume(j, slot, diag):
        v1 = jnp.concatenate([v_ref[pl.ds(pl.multiple_of(j * tk, tk), tk), :], ones], axis=1)
        for g in range(groups):
            rows = slice(g * rg, (g + 1) * rg)
            kw = (g + 1) * rg if diag else tk
            s = s_sc[slot, rows, :kw]
            if diag:
                qpos = g * rg + lax.broadcasted_iota(jnp.int32, (rg, kw), 0)
                kpos = lax.broadcasted_iota(jnp.int32, (rg, kw), 1)
                s = jnp.where(kpos <= qpos, s, NEG)
            m_prev = m_sc[rows, :]
            m_new = jnp.maximum(m_prev, jnp.max(s, axis=-1, keepdims=True))
            alpha = jnp.exp2(m_prev - m_new)
            p = jnp.exp2((s - m_new).astype(BF16))
            acc_sc[rows, :] = alpha * acc_sc[rows, :] + _dot(p, v1[:kw])
            m_sc[rows, :] = m_new

    scores(i, 1, diag=True)

    @pl.when(i == 0)
    def _():
        consume(i, 1, True)

    @pl.when(i > 0)
    def _():
        scores(0, 0)
        consume(i, 1, True)

    def pair(jj, c):
        j = 2 * jj
        scores(j + 1, 1)
        consume(j, 0, False)
        scores(j + 2, 0)
        consume(j + 1, 1, False)
        return c

    n_pairs = jnp.maximum(i - 1, 0) // 2
    lax.fori_loop(0, n_pairs, pair, 0)
    r = 2 * n_pairs

    @pl.when(i - r == 1)
    def _():
        consume(r, 0, False)

    @pl.when(i - r == 2)
    def _():
        scores(r + 1, 1)
        consume(r, 0, False)
        consume(r + 1, 1, False)

    o_ref[...] = (acc_sc[:, :dv] / acc_sc[:, dv:]).astype(o_ref.dtype)


def flash_attention(q_arr, k_arr, v_arr, kbias, *, heads, dk, dv, q_off, k_off, v_off, tq, tk, groups=8):
    B, S = q_arr.shape[:2]
    assert S % tk == 0 and tk == tq and tq % groups == 0
    has_bias = kbias is not None
    in_specs = [pl.BlockSpec((None, tq, dk), lambda b, h, i: (b, i, q_off + h)),
                pl.BlockSpec((None, S, dk), lambda b, h, i: (b, 0, k_off + h)),
                pl.BlockSpec((None, S, dv), lambda b, h, i: (b, 0, v_off + h))]
    args = [q_arr, k_arr, v_arr]
    if has_bias:
        in_specs.append(pl.BlockSpec((None, None, S // tk, tk), lambda b, h, i: (b, h, 0, 0)))
        args.append(kbias)
    return pl.pallas_call(
        functools.partial(_flash_kernel, tq=tq, tk=tk, groups=groups, has_bias=has_bias),
        grid=(B, heads, S // tq),
        in_specs=in_specs,
        out_specs=pl.BlockSpec((None, tq, dv), lambda b, h, i: (b, i, h)),
        out_shape=jax.ShapeDtypeStruct((B, S, heads * dv), BF16),
        scratch_shapes=[pltpu.VMEM((tq, 1), F32), pltpu.VMEM((tq, dv + LANES), F32),
                        pltpu.VMEM((2, tq, tk), F32)],
        compiler_params=_params("parallel", "parallel", "arbitrary"),
        name="flash_attention",
    )(*args)


def _attn_tiles(S):
    tk = _tile(S, 1024)
    return tk, tk


def _swa_kernel(sink_ref, q_ref, kvc_ref, kvp_ref, bias_ref, o_ref):
    i = pl.program_id(1)
    lane = lax.broadcasted_iota(jnp.int32, (1, 2 * WINDOW), 1)
    pen = jnp.where((lane < WINDOW) & (i == 0), NEG, 0.0).astype(F32)
    for g in range(SWA_KV_HEADS):
        kcat = jnp.concatenate([kvp_ref[:, g * 128:(g + 1) * 128], kvc_ref[:, g * 128:(g + 1) * 128]], axis=0)
        vcat = jnp.concatenate([kvp_ref[:, 256 + g * 128:256 + (g + 1) * 128],
                                kvc_ref[:, 256 + g * 128:256 + (g + 1) * 128]], axis=0)
        for hh in range(SWA_GROUP):
            h = g * SWA_GROUP + hh
            s = _dot_nt(q_ref[:, h * 128:(h + 1) * 128], kcat) * (SWA_HEAD_DIM ** -0.5) + bias_ref[h] + pen
            sink = sink_ref[h]
            m = jnp.maximum(jnp.max(s, axis=-1, keepdims=True), sink)
            p = jnp.exp(s - m)
            denom = jnp.sum(p, axis=-1, keepdims=True) + jnp.exp(sink - m)
            o = _dot(p.astype(BF16), vcat) / denom
            o_ref[:, h * 128:(h + 1) * 128] = o.astype(o_ref.dtype)


def swa_attention(z, bias_tab, sinks, B, S):
    T = z.shape[0]
    nb = S // WINDOW
    return pl.pallas_call(
        _swa_kernel,
        grid=(B, nb),
        in_specs=[pl.BlockSpec(memory_space=pltpu.SMEM),
                  pl.BlockSpec((WINDOW, 2048), lambda b, i: (b * nb + i, 0)),
                  pl.BlockSpec((WINDOW, 512), lambda b, i: (b * nb + i, 4)),
                  pl.BlockSpec((WINDOW, 512), lambda b, i: (b * nb + jnp.maximum(i - 1, 0), 4)),
                  pl.BlockSpec(bias_tab.shape, lambda b, i: (0, 0, 0))],
        out_specs=pl.BlockSpec((WINDOW, 2048), lambda b, i: (b * nb + i, 0)),
        out_shape=jax.ShapeDtypeStruct((T, 2048), BF16),
        compiler_params=_params("parallel", "arbitrary"),
        name="swa_attention",
    )(sinks, z, z, z, bias_tab)


def _decay_kernel(f_ref, b_ref, o_ref, carry_sc, *, tm):
    @pl.when(pl.program_id(1) == 0)
    def _():
        carry_sc[...] = jnp.zeros(carry_sc.shape, F32)

    x = f_ref[...] + b_ref[...]
    lf = jnp.minimum(x, 0.0) - jnp.log1p(jnp.exp(-jnp.abs(x)))
    row = lax.broadcasted_iota(jnp.int32, (tm, tm), 0)
    col = lax.broadcasted_iota(jnp.int32, (tm, tm), 1)
    tri = jnp.where(col <= row, 1.0, 0.0).astype(BF16)
    hi, mid, lo = _split3(lf)
    cs = _dot(tri, hi) + _dot(tri, mid) + _dot(tri, lo) + carry_sc[...]
    carry_sc[...] = cs[tm - 1:tm, :]
    o_ref[...] = (cs * LOG2E).T


def decay_cumsum(f_logit, b_f, B, S, tm=512):
    tm = _tile(S, tm)
    ns = S // tm
    return pl.pallas_call(
        functools.partial(_decay_kernel, tm=tm),
        grid=(B, ns),
        in_specs=[pl.BlockSpec((tm, LANES), lambda b, i: (b * ns + i, 0)),
                  pl.BlockSpec((1, LANES), lambda b, i: (0, 0))],
        out_specs=pl.BlockSpec((None, LANES, tm), lambda b, i: (b, 0, i)),
        out_shape=jax.ShapeDtypeStruct((B, LANES, S), F32),
        scratch_shapes=[pltpu.VMEM((1, LANES), F32)],
        compiler_params=_params("parallel", "arbitrary"),
        name="decay_cumsum",
    )(f_logit, b_f)


def _cross_kernel(x_ref, g_ref, wq_ref, k_ref, v_ref, wo_ref, o_ref):
    x = x_ref[...]
    hn = _rms(x, g_ref[...]).astype(BF16)
    q = (_dot(hn, wq_ref[...]) * (X_HEAD_DIM ** -0.5)).astype(BF16)
    outs = []
    for h in range(X_HEADS):
        sl = slice(h * X_HEAD_DIM, (h + 1) * X_HEAD_DIM)
        s = _dot_nt(q[:, sl], k_ref[:, sl])
        m = jnp.max(s, axis=-1, keepdims=True)
        p = jnp.exp(s - m)
        o = _dot(p.astype(BF16), v_ref[:, sl]) / jnp.sum(p, axis=-1, keepdims=True)
        outs.append(o.astype(BF16))
    o_ref[...] = x + _dot(jnp.concatenate(outs, axis=1), wo_ref[...])


def cross_attention(x, g, wq, kv, wo, B, S, tm=512):
    T, D = x.shape
    M = kv.shape[1]
    XW = X_HEADS * X_HEAD_DIM
    tm = _tile(S, tm)
    ns = S // tm
    return pl.pallas_call(
        _cross_kernel,
        grid=(B, ns),
        in_specs=[pl.BlockSpec((tm, D), lambda b, i: (b * ns + i, 0)),
                  pl.BlockSpec((1, D), lambda b, i: (0, 0)),
                  pl.BlockSpec((D, XW), lambda b, i: (0, 0)),
                  pl.BlockSpec((None, M, XW), lambda b, i: (b, 0, 0)),
                  pl.BlockSpec((None, M, XW), lambda b, i: (b, 0, 1)),
                  pl.BlockSpec((XW, D), lambda b, i: (0, 0))],
        out_specs=pl.BlockSpec((tm, D), lambda b, i: (b * ns + i, 0)),
        out_shape=jax.ShapeDtypeStruct((T, D), F32),
        compiler_params=_params("parallel", "parallel"),
        name="cross_attention",
    )(x, g.reshape(1, D), wq, kv, kv, wo)


def _ffn_kernel(x_ref, g_ref, wg_ref, wu_ref, wd_ref, o_ref, h_sc):
    @pl.when(pl.program_id(1) == 0)
    def _():
        x = x_ref[...]
        h_sc[...] = _rms(x, g_ref[...]).astype(BF16)
        o_ref[...] = x

    h = h_sc[...]
    gate = _dot(h, wg_ref[...])
    up = _dot(h, wu_ref[...])
    a = (gate * jax.nn.sigmoid(gate) * up).astype(BF16)
    o_ref[...] += _dot(a, wd_ref[...])


def ffn(x, g, wg, wu, wd, tm=512, tf=1024):
    T, D = x.shape
    F = wg.shape[1]
    tm, tf = _tile(T, tm), _tile(F, tf)
    return pl.pallas_call(
        _ffn_kernel,
        grid=(T // tm, F // tf),
        in_specs=[pl.BlockSpec((tm, D), lambda i, f: (i, 0)),
                  pl.BlockSpec((1, D), lambda i, f: (0, 0)),
                  pl.BlockSpec((D, tf), lambda i, f: (0, f)),
                  pl.BlockSpec((D, tf), lambda i, f: (0, f)),
                  pl.BlockSpec((tf, D), lambda i, f: (f, 0))],
        out_specs=pl.BlockSpec((tm, D), lambda i, f: (i, 0)),
        out_shape=jax.ShapeDtypeStruct((T, D), F32),
        scratch_shapes=[pltpu.VMEM((tm, D), BF16)],
        compiler_params=_params("parallel", "arbitrary"),
        name="ffn",
    )(x, g.reshape(1, D), wg, wu, wd)


def _router_kernel(x_ref, g_ref, wh_ref, wl_ref, b_ref, mi_ref, mg_ref, cnt_ref, carry_sc, *, tm):
    @pl.when(pl.program_id(0) == 0)
    def _():
        carry_sc[...] = jnp.zeros(carry_sc.shape, F32)

    hn = _rms(x_ref[...], g_ref[...])
    h_hi = hn.astype(BF16)
    h_lo = (hn - h_hi.astype(F32)).astype(BF16)
    logits = (_dot(h_hi, wh_ref[...]) + _dot(h_lo, wh_ref[...]) + _dot(h_hi, wl_ref[...])) + b_ref[...]
    lane = lax.broadcasted_iota(jnp.int32, logits.shape, 1)
    logits = jnp.where(lane < N_EXPERTS, logits, -jnp.inf)
    m1 = jnp.max(logits, axis=-1, keepdims=True)
    i1 = jnp.min(jnp.where(logits == m1, lane, LANES), axis=-1, keepdims=True)
    oh1 = lane == i1
    l2 = jnp.where(oh1, -jnp.inf, logits)
    m2 = jnp.max(l2, axis=-1, keepdims=True)
    i2 = jnp.min(jnp.where(l2 == m2, lane, LANES), axis=-1, keepdims=True)
    oh2 = lane == i2
    e2 = jnp.exp(m2 - m1)
    g1 = 1.0 / (1.0 + e2)
    mg_ref[...] = jnp.where(lane == 0, g1, jnp.where(lane == 1, e2 * g1, 0.0))
    row = lax.broadcasted_iota(jnp.int32, (tm, tm), 0)
    col = lax.broadcasted_iota(jnp.int32, (tm, tm), 1)
    tri = jnp.where(col < row, 1.0, 0.0).astype(BF16)
    oh1f = jnp.where(oh1, 1.0, 0.0)
    oh2f = jnp.where(oh2, 1.0, 0.0)
    ex1 = _dot(tri, oh1f.astype(BF16))
    ex2 = _dot(tri, oh2f.astype(BF16))
    tot1 = jnp.sum(oh1f, axis=0, keepdims=True)
    tot2 = jnp.sum(oh2f, axis=0, keepdims=True)
    carry = carry_sc[...]
    r1 = jnp.sum(jnp.where(oh1, ex1 + carry, 0.0), axis=-1, keepdims=True).astype(jnp.int32)
    r2 = jnp.sum(jnp.where(oh2, ex2 + (carry + tot1), 0.0), axis=-1, keepdims=True).astype(jnp.int32)
    mi_ref[...] = jnp.where(lane == 0, i1, jnp.where(lane == 1, i2, jnp.where(lane == 2, r1,
                            jnp.where(lane == 3, r2, 0))))
    total = carry + tot1 + tot2
    carry_sc[...] = total
    cnt_ref[...] = total


def router(x, g, w_router, b_router, tm=512):
    T, D = x.shape
    tm = _tile(T, tm)
    wp = jnp.pad(w_router, ((0, 0), (0, LANES - N_EXPERTS)))
    w_hi = wp.astype(BF16)
    w_lo = (wp - w_hi.astype(F32)).astype(BF16)
    bp = jnp.pad(b_router, (0, LANES - N_EXPERTS)).reshape(1, LANES)
    return pl.pallas_call(
        functools.partial(_router_kernel, tm=tm),
        grid=(T // tm,),
        in_specs=[pl.BlockSpec((tm, D), lambda i: (i, 0)),
                  pl.BlockSpec((1, D), lambda i: (0, 0)),
                  pl.BlockSpec((D, LANES), lambda i: (0, 0)),
                  pl.BlockSpec((D, LANES), lambda i: (0, 0)),
                  pl.BlockSpec((1, LANES), lambda i: (0, 0))],
        out_specs=[pl.BlockSpec((tm, LANES), lambda i: (i, 0)),
                   pl.BlockSpec((tm, LANES), lambda i: (i, 0)),
                   pl.BlockSpec((1, LANES), lambda i: (0, 0))],
        out_shape=[jax.ShapeDtypeStruct((T, LANES), jnp.int32),
                   jax.ShapeDtypeStruct((T, LANES), F32),
                   jax.ShapeDtypeStruct((1, LANES), F32)],
        scratch_shapes=[pltpu.VMEM((1, LANES), F32)],
        compiler_params=_params("arbitrary"),
        name="router",
    )(x, g.reshape(1, D), w_hi, w_lo, bp)


def _row_copy(src_ref, src_row, dst_ref, dst_row, sem):
    return pltpu.make_async_copy(src_ref.at[pl.ds(src_row, 1), :], dst_ref.at[pl.ds(dst_row, 1), :], sem)


def _dispatch_kernel(grp_ref, pos_ref, x_ref, xs_ref, zero_sc, sem, *, tm):
    def issue(r, c):
        _row_copy(x_ref, r, xs_ref, pos_ref[0, 2 * r], sem).start(priority=0)
        _row_copy(x_ref, r, xs_ref, pos_ref[0, 2 * r + 1], sem).start(priority=1)
        return c

    lax.fori_loop(0, tm, issue, 0, unroll=8)

    def pad_rows(fn):
        for e in range(N_EXPERTS):
            cnt, padded, off = grp_ref[e], grp_ref[N_EXPERTS + e], grp_ref[2 * N_EXPERTS + e]
            lax.fori_loop(cnt, padded, lambda r, c: fn(off + r, c), 0)
        lax.fori_loop(grp_ref[3 * N_EXPERTS], xs_ref.shape[0], fn, 0)

    @pl.when(pl.program_id(0) == 0)
    def _():
        zero_sc[...] = jnp.zeros(zero_sc.shape, F32)

        def start(p, c):
            _row_copy(zero_sc, 0, xs_ref, p, sem).start()
            return c

        def wait(p, c):
            _row_copy(zero_sc, 0, xs_ref, p, sem).wait()
            return c

        pad_rows(start)
        pad_rows(wait)

    def drain(r, c):
        _row_copy(x_ref, r, xs_ref, pos_ref[0, 2 * r], sem).wait()
        _row_copy(x_ref, r, xs_ref, pos_ref[0, 2 * r + 1], sem).wait()
        return c

    lax.fori_loop(0, tm, drain, 0, unroll=8)


def moe_dispatch(x, pos, grp, n_rows, tm=512):
    T, D = x.shape
    tm = _tile(T, tm)
    pos3 = pos.reshape(T // tm, 1, 2 * tm)
    return pl.pallas_call(
        functools.partial(_dispatch_kernel, tm=tm),
        grid_spec=pltpu.PrefetchScalarGridSpec(
            num_scalar_prefetch=1,
            grid=(T // tm,),
            in_specs=[pl.BlockSpec((None, 1, 2 * tm), lambda i, grp: (i, 0, 0), memory_space=pltpu.SMEM),
                      pl.BlockSpec((tm, D), lambda i, grp: (i, 0))],
            out_specs=pl.BlockSpec(memory_space=pl.ANY),
            scratch_shapes=[pltpu.VMEM((8, D), F32), pltpu.SemaphoreType.DMA(())]),
        out_shape=jax.ShapeDtypeStruct((n_rows, D), F32),
        compiler_params=_params("arbitrary"),
        name="moe_dispatch",
    )(grp, pos3, x)


def _moe_expert_kernel(te_ref, nact_ref, xs_ref, g_ref, wg_ref, wu_ref, wd_ref, y_ref, h_sc):
    i, f = pl.program_id(0), pl.program_id(1)
    active = i < nact_ref[0]

    @pl.when(active & (f == 0))
    def _():
        h_sc[...] = _rms(xs_ref[...], g_ref[...]).astype(BF16)

    @pl.when(active)
    def _():
        h = h_sc[...]
        gate = _dot(h, wg_ref[...])
        up = _dot(h, wu_ref[...])
        a = (gate * jax.nn.sigmoid(gate) * up).astype(BF16)
        contrib = _dot(a, wd_ref[...])

        @pl.when(f == 0)
        def _():
            y_ref[...] = contrib

        @pl.when(f != 0)
        def _():
            y_ref[...] += contrib

    @pl.when(jnp.logical_not(active) & (f == 0))
    def _():
        y_ref[...] = jnp.zeros(y_ref.shape, F32)


def moe_experts(xs, g, tile_expert, n_active, wg, wu, wd, tm, tf=1024):
    P, D = xs.shape
    F = wg.shape[2]
    tf = _tile(F, tf)
    nf = F // tf
    row = lambda i, f, te, na: (jnp.minimum(i, na[0] - 1), 0)
    fcol = lambda i, f, na: jnp.where(i < na[0], f, nf - 1)
    return pl.pallas_call(
        _moe_expert_kernel,
        grid_spec=pltpu.PrefetchScalarGridSpec(
            num_scalar_prefetch=2,
            grid=(P // tm, nf),
            in_specs=[pl.BlockSpec((tm, D), row),
                      pl.BlockSpec((1, D), lambda i, f, te, na: (0, 0)),
                      pl.BlockSpec((None, D, tf), lambda i, f, te, na: (te[i], 0, fcol(i, f, na))),
                      pl.BlockSpec((None, D, tf), lambda i, f, te, na: (te[i], 0, fcol(i, f, na))),
                      pl.BlockSpec((None, tf, D), lambda i, f, te, na: (te[i], fcol(i, f, na), 0))],
            out_specs=pl.BlockSpec((tm, D), lambda i, f, te, na: (i, 0)),
            scratch_shapes=[pltpu.VMEM((tm, D), BF16)]),
        out_shape=jax.ShapeDtypeStruct((P, D), F32),
        compiler_params=_params("arbitrary", "arbitrary"),
        name="moe_experts",
    )(tile_expert, n_active, xs, g.reshape(1, D), wg, wu, wd)


def _combine_kernel(pos_ref, x_ref, mg_ref, gf_ref, y_ref, o_ref, ybuf, sem, *, tm):
    def issue(r, c):
        _row_copy(y_ref, pos_ref[0, 2 * r], ybuf.at[0], r, sem).start(priority=0)
        _row_copy(y_ref, pos_ref[0, 2 * r + 1], ybuf.at[1], r, sem).start(priority=1)
        return c

    def drain(r, c):
        _row_copy(y_ref, pos_ref[0, 2 * r], ybuf.at[0], r, sem).wait()
        _row_copy(y_ref, pos_ref[0, 2 * r + 1], ybuf.at[1], r, sem).wait()
        return c

    lax.fori_loop(0, tm, issue, 0, unroll=8)
    lax.fori_loop(0, tm, drain, 0, unroll=8)
    mg = mg_ref[...]
    v = x_ref[...] + mg[:, 0:1] * ybuf[0] + mg[:, 1:2] * ybuf[1]
    o_ref[...] = _rms(v, gf_ref[...])


def moe_combine_norm(x, y, pos, gates, final_g, tm=512):
    T, D = x.shape
    tm = _tile(T, tm)
    pos3 = pos.reshape(T // tm, 1, 2 * tm)
    return pl.pallas_call(
        functools.partial(_combine_kernel, tm=tm),
        grid=(T // tm,),
        in_specs=[pl.BlockSpec((None, 1, 2 * tm), lambda i: (i, 0, 0), memory_space=pltpu.SMEM),
                  pl.BlockSpec((tm, D), lambda i: (i, 0)),
                  pl.BlockSpec((tm, LANES), lambda i: (i, 0)),
                  pl.BlockSpec((1, D), lambda i: (0, 0)),
                  pl.BlockSpec(memory_space=pl.ANY)],
        out_specs=pl.BlockSpec((tm, D), lambda i: (i, 0)),
        out_shape=jax.ShapeDtypeStruct((T, D), F32),
        scratch_shapes=[pltpu.VMEM((2, tm, D), F32), pltpu.SemaphoreType.DMA(())],
        compiler_params=_params("arbitrary"),
        name="moe_combine_norm",
    )(pos3, x, gates, final_g.reshape(1, D), y)


def routed_moe_norm(x, g, w_router, b_router, wg, wu, wd, final_g, tm_e=512):
    T, D = x.shape
    tm_e = _tile(T, tm_e)
    mi, mg, cnt = router(x, g, w_router, b_router)
    cnt = cnt[0, :N_EXPERTS].astype(jnp.int32)
    padded = (cnt + tm_e - 1) // tm_e * tm_e
    ends = jnp.cumsum(padded)
    off = ends - padded
    first = lambda e: jnp.sum(jnp.where(e[:, None] == jnp.arange(N_EXPERTS)[None, :], off[None, :], 0), axis=1)
    pos = jnp.stack([first(mi[:, 0]) + mi[:, 2], first(mi[:, 1]) + mi[:, 3]], axis=1)
    n_rows = 2 * T + N_EXPERTS * tm_e
    n_active = (ends[-1:] // tm_e).astype(jnp.int32)
    starts = jnp.minimum(jnp.arange(n_rows // tm_e, dtype=jnp.int32), n_active - 1) * tm_e
    tile_expert = jnp.sum(starts[:, None] >= ends[None, :], axis=1).astype(jnp.int32)
    grp = jnp.concatenate([cnt, padded, off, ends[-1:]]).astype(jnp.int32)
    xs = moe_dispatch(x, pos, grp, n_rows)
    y = moe_experts(xs, g, tile_expert, n_active, wg, wu, wd, tm_e)
    return moe_combine_norm(x, y, pos, mg, final_g)


def _pad_heads(w, heads, dim, to=LANES):
    lead = w.shape[:-1]
    w = w.reshape(lead + (heads, dim))
    w = jnp.pad(w, [(0, 0)] * len(lead) + [(0, 0), (0, to - dim)])
    return w.reshape(lead + (heads * to,))


def _swap_halves(w):
    half = w.shape[-1] // 2
    return jnp.concatenate([w[..., half:], w[..., :half]], axis=-1)


def _even_w_in(w):
    c_q = w[:, :MLA_Q_RANK]
    o = MLA_Q_RANK
    c_kv = w[:, o:o + MLA_KV_RANK]
    o += MLA_KV_RANK
    k_rope = w[:, o:o + MLA_ROPE]
    o += MLA_ROPE
    q_s = w[:, o:o + SWA_HEADS * SWA_HEAD_DIM]
    o += SWA_HEADS * SWA_HEAD_DIM
    k_s = w[:, o:o + SWA_KV_HEADS * SWA_HEAD_DIM]
    o += SWA_KV_HEADS * SWA_HEAD_DIM
    v_s = w[:, o:o + SWA_KV_HEADS * SWA_HEAD_DIM]
    pad = lambda a: jnp.pad(a, ((0, 0), (0, LANES - a.shape[1])))
    return jnp.concatenate([
        _pad_heads(q_s, SWA_HEADS, SWA_HEAD_DIM),
        _pad_heads(k_s, SWA_KV_HEADS, SWA_HEAD_DIM),
        _pad_heads(v_s, SWA_KV_HEADS, SWA_HEAD_DIM),
        c_q, c_kv, pad(k_rope), pad(_swap_halves(k_rope))], axis=1).astype(BF16)


def _t5_bucket(dist):
    max_exact = REL_BUCKETS // 2
    large = max_exact + (jnp.log(jnp.maximum(dist, 1).astype(F32) / max_exact)
                         / math.log(REL_MAX_DIST / max_exact) * (REL_BUCKETS - max_exact)).astype(jnp.int32)
    large = jnp.minimum(large, REL_BUCKETS - 1)
    return jnp.where(dist < max_exact, dist, large)


def _swa_bias_table(rel_bias):
    i = jnp.arange(WINDOW)[:, None]
    j = jnp.arange(2 * WINDOW)[None, :]
    rel = WINDOW + i - j
    onehot = (_t5_bucket(jnp.maximum(rel, 0))[..., None] == jnp.arange(REL_BUCKETS)).astype(F32)
    bias = jnp.einsum('ijb,bh->ijh', onehot, rel_bias.astype(F32), precision=lax.Precision.HIGHEST)
    valid = (rel >= 0) & (rel < WINDOW)
    return jnp.where(valid[None], jnp.transpose(bias, (2, 0, 1)), NEG)


def _rope_tables(positions):
    half = MLA_ROPE // 2
    inv = ROPE_THETA ** (-jnp.arange(half, dtype=F32) / half)
    ang = positions.astype(F32).reshape(-1, 1) * inv
    cos, sin = jnp.cos(ang), jnp.sin(ang)
    z = jnp.zeros((ang.shape[0], LANES - MLA_ROPE), F32)
    return jnp.concatenate([cos, cos, z], axis=1), jnp.concatenate([-sin, sin, z], axis=1)


def kernel(x, mem, positions, rel_bias, norm_mix, norm_cross, norm_mem, norm_ffn, x_wq, x_wk, x_wv, x_wo,
           ev_w_in, ev_mla_q_norm, ev_mla_w_uq, ev_mla_kv_norm, ev_mla_w_ukv, ev_swa_sinks, ev_w_o,
           ev_ffn_w_gate, ev_ffn_w_up, ev_ffn_w_down, od_w_in, od_fox_b_f, od_w_o, od_router_w, od_router_b,
           od_moe_w_gate, od_moe_w_up, od_moe_w_down, final_norm_g):
    B, S, D = x.shape
    T = B * S
    M = mem.shape[1]
    xs = x.reshape(T, D)
    mem2 = mem.reshape(B * M, D)
    bf = lambda a: a.astype(BF16)

    def cross(xs, layer):
        kvw = bf(jnp.concatenate([x_wk[layer], x_wv[layer]], axis=1))
        kv = norm_matmul(mem2, norm_mem[layer], kvw, tm=256, tn=1024).reshape(B, M, -1)
        return cross_attention(xs, norm_cross[layer], bf(x_wq[layer]), kv, bf(x_wo[layer]), B, S)

    z0 = norm_matmul(xs, norm_mix[0], _even_w_in(ev_w_in[0]), tm=512, tn=2048)
    cos2, sin2 = _rope_tables(positions)
    w_uq = ev_mla_w_uq[0].reshape(MLA_Q_RANK, MLA_HEADS, MLA_NOPE + MLA_ROPE)
    rope_w = w_uq[..., MLA_NOPE:]
    padr = lambda a: jnp.pad(a, ((0, 0), (0, 0), (0, LANES - MLA_ROPE)))
    wq_all = bf(jnp.concatenate([w_uq[..., :MLA_NOPE], padr(rope_w)], axis=-1).reshape(MLA_Q_RANK, -1))
    wq_swap = bf(padr(_swap_halves(rope_w)).reshape(MLA_Q_RANK, -1))
    w_ukv = ev_mla_w_ukv[0].reshape(MLA_KV_RANK, MLA_HEADS, MLA_NOPE + MLA_V)
    wk = bf(w_ukv[..., :MLA_NOPE].reshape(MLA_KV_RANK, -1))
    wv = bf(w_ukv[..., MLA_NOPE:].reshape(MLA_KV_RANK, -1))
    q, k, v = mla_prep(z0, cos2, sin2, ev_mla_q_norm[0], ev_mla_kv_norm[0], wq_all, wq_swap, wk, wv)
    tq, tk = _attn_tiles(S)
    o_mla = flash_attention(q.reshape(B, S, -1), k.reshape(B, S, -1), v.reshape(B, S, -1), None,
                            heads=MLA_HEADS, dk=256, dv=MLA_V, q_off=0, k_off=0, v_off=0, tq=tq, tk=tk)
    o_swa = swa_attention(z0, _swa_bias_table(rel_bias), ev_swa_sinks[0], B, S)
    n_mla = MLA_HEADS * MLA_V
    w_o_swa = ev_w_o[0][n_mla:].reshape(SWA_HEADS, SWA_HEAD_DIM, D)
    w_o_swa = jnp.pad(w_o_swa, ((0, 0), (0, LANES - SWA_HEAD_DIM), (0, 0))).reshape(SWA_HEADS * LANES, D)
    xs = matmul_res(xs, [(o_mla.reshape(T, n_mla), bf(ev_w_o[0][:n_mla])), (o_swa, bf(w_o_swa))])
    xs = cross(xs, 0)
    xs = ffn(xs, norm_ffn[0], bf(ev_ffn_w_gate[0]), bf(ev_ffn_w_up[0]), bf(ev_ffn_w_down[0]))

    FW = FOX_HEADS * FOX_HEAD_DIM
    col_scale = jnp.concatenate([jnp.full((1, FW), LOG2E * FOX_HEAD_DIM ** -0.5, F32), jnp.ones((1, 2 * FW), F32)], axis=1)
    w_f = jnp.pad(od_w_in[0][:, 3 * FW:], ((0, 0), (0, LANES - FOX_HEADS)))
    z1, f_logit = norm_matmul(xs, norm_mix[1], bf(od_w_in[0][:, :3 * FW]), col_scale=col_scale, side_w=w_f,
                              tm=512, tn=2048)
    b_f = jnp.pad(od_fox_b_f[0], (0, LANES - FOX_HEADS)).reshape(1, LANES)
    decay = decay_cumsum(f_logit, b_f, B, S)[:, :FOX_HEADS, :].reshape(B, FOX_HEADS, S // tk, tk)
    z1 = z1.reshape(B, S, 3 * FW)
    o_fox = flash_attention(z1, z1, z1, decay, heads=FOX_HEADS, dk=FOX_HEAD_DIM, dv=FOX_HEAD_DIM,
                            q_off=0, k_off=FOX_HEADS, v_off=2 * FOX_HEADS, tq=tq, tk=tk)
    xs = matmul_res(xs, [(o_fox.reshape(T, FW), bf(od_w_o[0]))])
    xs = cross(xs, 1)
    out = routed_moe_norm(xs, norm_ffn[1], od_router_w[0], od_router_b[0], bf(od_moe_w_gate[0]),
                          bf(od_moe_w_up[0]), bf(od_moe_w_down[0]), final_norm_g)
    return out.reshape(B, S, D)
```
